```python
import math
import jax
import jax.numpy as jnp
from jax import lax
import numpy as np

D_MODEL = 2048
BATCH = 4
SEQ = 2048
DEPTH = 2
DEC_BATCH = 32
DEC_SEQ = 1
PAST_LEN = 8192
PAGE_SIZE = 128

SB_HEADS = 8
SB_DIM = 64
SB_WIDTH = SB_HEADS * SB_DIM
LRU_WIDTH = D_MODEL // 4
LRU_BLOCKS = 8
LRU_BLOCK = LRU_WIDTH // LRU_BLOCKS
LRU_CONV = 4
LRU_C = 8.0
DA_HEADS = 8
DA_DIM = 64
DA_VDIM = 2 * DA_DIM
DA_WIDTH = DA_HEADS * DA_VDIM
D_FF = 5632
FFN_CONV = 3
N_BRANCH = 3
ROPE_THETA = 10000.0
NORM_EPS = 1e-6
SUBLN_EPS = 1e-5
Q_BLOCK = 128
NEG_INF = -1e30
IN_SIZES = (SB_WIDTH, SB_WIDTH, SB_WIDTH, LRU_WIDTH, DA_WIDTH, DA_WIDTH, DA_WIDTH, N_BRANCH * D_MODEL)
IN_COLS = sum(IN_SIZES)
IN_SPLITS = tuple(int(c) for c in np.cumsum(IN_SIZES)[:-1])

kernel_name = 'hybrid_stickbreak_rglru_diffattn_step'


def rmsnorm(x, g, eps=NORM_EPS):
    xf = x.astype(jnp.float32)
    y = xf * lax.rsqrt(jnp.mean(xf * xf, axis=-1, keepdims=True) + eps)
    return (y * g.astype(jnp.float32)).astype(x.dtype)


def causal_dwconv(x, buf, w, b):
    width = w.shape[0]
    t = x.shape[1]
    xe = jnp.concatenate([buf.astype(x.dtype), x], axis=1)
    y = b
    for j in range(width):
        y = y + xe[:, j:j + t] * w[j]
    return y.astype(x.dtype), xe[:, xe.shape[1] - (width - 1):]


def rope(x, pos):
    half = x.shape[-1] // 2
    inv_freq = ROPE_THETA ** (-jnp.arange(half, dtype=jnp.float32) / half)
    ang = pos.astype(jnp.float32)[:, None] * inv_freq[None, :]
    cos = jnp.cos(ang)[None, :, None, :]
    sin = jnp.sin(ang)[None, :, None, :]
    xf = x.astype(jnp.float32)
    x1, x2 = xf[..., :half], xf[..., half:]
    return jnp.concatenate([x1 * cos - x2 * sin, x2 * cos + x1 * sin], axis=-1).astype(x.dtype)


def _scores(q, k_parts):
    return jnp.concatenate([jnp.einsum('bqhd,bkhd->bhqk', q, k, preferred_element_type=jnp.float32)
                            for k in k_parts], axis=-1)


def _apply(w, v_parts):
    out, start = None, 0
    for v in v_parts:
        n = v.shape[1]
        o = jnp.einsum('bhqk,bkhd->bqhd', w[..., start:start + n].astype(v.dtype), v)
        out = o if out is None else out + o
        start += n
    return out


def stick_breaking(q, q_pos, k_parts, v_parts, k_pos):
    z = _scores(q, k_parts) * (SB_DIM ** -0.5)
    mask = k_pos[None, :] < q_pos[:, None]
    log_keep = jnp.where(mask, jax.nn.log_sigmoid(-z), 0.0)
    after = lax.cumsum(log_keep, axis=3, reverse=True) - log_keep
    w = jnp.where(mask, jnp.exp(jax.nn.log_sigmoid(z) + after), 0.0)
    return _apply(w, v_parts)


def diff_attention(q, q_pos, k_parts, v_parts, k_pos, lam):
    s = _scores(q, k_parts) * (DA_DIM ** -0.5)
    mask = k_pos[None, :] <= q_pos[:, None]
    p = jax.nn.softmax(jnp.where(mask, s, NEG_INF), axis=-1)
    b, _, tq, sk = p.shape
    p = p.reshape(b, DA_HEADS, 2, tq, sk)
    w = p[:, :, 0] - lam * p[:, :, 1]
    return _apply(w, v_parts)


def blocked_prompt(attn, q, k, v, *extra):
    t = q.shape[1]
    pos = jnp.arange(t)
    outs = []
    for lo in range(0, t, Q_BLOCK):
        hi = min(lo + Q_BLOCK, t)
        outs.append(attn(q[:, lo:hi], pos[lo:hi], [k[:, :hi]], [v[:, :hi]], pos[:hi], *extra))
    return jnp.concatenate(outs, axis=1)


def rg_lru(x, h0, wa, ba, wx, bx, lam):
    b, t, _ = x.shape
    xb = x.reshape(b, t, LRU_BLOCKS, LRU_BLOCK)
    r = jax.nn.sigmoid((jnp.einsum('btnj,njk->btnk', xb, wa).reshape(b, t, LRU_WIDTH) + ba).astype(jnp.float32))
    i = jax.nn.sigmoid((jnp.einsum('btnj,njk->btnk', xb, wx).reshape(b, t, LRU_WIDTH) + bx).astype(jnp.float32))
    log_a = LRU_C * r * jax.nn.log_sigmoid(lam.astype(jnp.float32))
    a = jnp.exp(log_a)
    u = jnp.sqrt(-jnp.expm1(2.0 * log_a)) * i * x.astype(jnp.float32)
    u = u.at[:, 0].add(a[:, 0] * h0)

    def combine(left, right):
        return left[0] * right[0], right[0] * left[1] + right[1]

    _, h = lax.associative_scan(combine, (a, u), axis=1)
    return h.astype(x.dtype), h[:, -1]


def _layer(x, pos, p, lam_init, past):
    bsz, t, _ = x.shape
    f32 = jnp.float32
    xn = rmsnorm(x, p['norm1_g'])
    sb_q, sb_k, sb_v, lru_in, da_q, da_k, da_v, gate_in = jnp.split(xn @ p['w_in'], IN_SPLITS, axis=-1)
    sb_q = sb_q.reshape(bsz, t, SB_HEADS, SB_DIM)
    sb_k = sb_k.reshape(bsz, t, SB_HEADS, SB_DIM)
    sb_v = sb_v.reshape(bsz, t, SB_HEADS, SB_DIM)
    da_q = rope(da_q.reshape(bsz, t, 2 * DA_HEADS, DA_DIM), pos)
    da_k = rope(da_k.reshape(bsz, t, 2 * DA_HEADS, DA_DIM), pos)
    da_v = da_v.reshape(bsz, t, DA_HEADS, DA_VDIM)
    lam = (jnp.exp(jnp.sum(p['lam_q1'].astype(f32) * p['lam_k1'].astype(f32)))
           - jnp.exp(jnp.sum(p['lam_q2'].astype(f32) * p['lam_k2'].astype(f32))) + lam_init)
    if past is None:
        y_sb = blocked_prompt(stick_breaking, sb_q, sb_k, sb_v)
        y_da = blocked_prompt(diff_attention, da_q, da_k, da_v, lam)
        lru_buf = jnp.zeros((bsz, LRU_CONV - 1, LRU_WIDTH), x.dtype)
        h0 = jnp.zeros((bsz, LRU_WIDTH), f32)
        ffn_buf = jnp.zeros((bsz, FFN_CONV - 1, D_FF), x.dtype)
    else:
        k_pos = jnp.arange(past['sb_k'].shape[1] + t)
        y_sb = stick_breaking(sb_q, pos, [past['sb_k'], sb_k], [past['sb_v'], sb_v], k_pos)
        y_da = diff_attention(da_q, pos, [past['da_k'], da_k], [past['da_v'], da_v], k_pos, lam)
        lru_buf = past['lru_conv']
        h0 = past['lru_h'].astype(f32)
        ffn_buf = past['ffn_conv']
    y_da = (rmsnorm(y_da, p['subln_g'], SUBLN_EPS) * (1.0 - lam_init)).reshape(bsz, t, DA_WIDTH)
    lru_c, new_lru_buf = causal_dwconv(lru_in, lru_buf, p['lru_conv_w'], p['lru_conv_b'])
    y_lru, h_last = rg_lru(lru_c, h0, p['lru_wa'], p['lru_ba'], p['lru_wx'], p['lru_bx'], p['lru_lambda'])
    g_a, g_b, g_c = jnp.split(jax.nn.sigmoid(gate_in), N_BRANCH, axis=-1)
    merged = (g_a * (y_sb.reshape(bsz, t, SB_WIDTH) @ p['w_br_a'])
              + g_b * (y_lru @ p['w_br_b'])
              + g_c * (y_da @ p['w_br_c']))
    x = x + merged @ p['w_out']
    xn2 = rmsnorm(x, p['norm2_g'])
    g, new_ffn_buf = causal_dwconv(xn2 @ p['ffn_w_gate'], ffn_buf, p['ffn_conv_w'], p['ffn_conv_b'])
    x = x + (jax.nn.silu(g) * (xn2 @ p['ffn_w_up'])) @ p['ffn_w_down']
    return x, (sb_k, sb_v, da_k, da_v, h_last.astype(x.dtype), new_lru_buf, new_ffn_buf)


def setup_inputs(seed: int = 0) -> dict:
    key = jax.random.key(seed)
    ks = jax.random.split(key, 40)
    f32 = jnp.float32

    def nrm(k, shape, scale):
        return jax.random.normal(k, shape, f32) * scale

    n_pages = PAST_LEN // PAGE_SIZE
    n_used = DEC_BATCH * n_pages
    n_pool = n_used + (n_used + 3) // 4
    page_table = jax.random.permutation(ks[0], n_pool)[:n_used].reshape(DEC_BATCH, n_pages).astype(jnp.int32)
    u = jax.random.uniform(ks[1], (DEPTH, LRU_WIDTH), f32, 0.9, 0.999)
    s = u ** (1.0 / LRU_C)
    lru_lambda = jnp.log(s) - jnp.log1p(-s)
    d_scale = D_MODEL ** -0.5
    return {
        'x_prompt': nrm(ks[2], (BATCH, SEQ, D_MODEL), 1.0),
        'x_sample': nrm(ks[3], (DEC_BATCH, DEC_SEQ, D_MODEL), 1.0),
        'cache_sb_k': nrm(ks[4], (DEPTH, n_pool, PAGE_SIZE, SB_HEADS, SB_DIM), 1.0),
        'cache_sb_v': nrm(ks[5], (DEPTH, n_pool, PAGE_SIZE, SB_HEADS, SB_DIM), 1.0),
        'cache_da_k': nrm(ks[6], (DEPTH, n_pool, PAGE_SIZE, 2 * DA_HEADS, DA_DIM), 1.0),
        'cache_da_v': nrm(ks[7], (DEPTH, n_pool, PAGE_SIZE, DA_HEADS, DA_VDIM), 1.0),
        'state_lru_h': nrm(ks[8], (DEPTH, DEC_BATCH, LRU_WIDTH), 0.5),
        'state_lru_conv': nrm(ks[9], (DEPTH, DEC_BATCH, LRU_CONV - 1, LRU_WIDTH), 1.0),
        'state_ffn_conv': nrm(ks[10], (DEPTH, DEC_BATCH, FFN_CONV - 1, D_FF), 1.0),
        'page_table': page_table,
        'norm1_g': 1.0 + nrm(ks[11], (DEPTH, D_MODEL), 0.02),
        'w_in': nrm(ks[12], (DEPTH, D_MODEL, IN_COLS), d_scale),
        'lru_conv_w': nrm(ks[13], (DEPTH, LRU_CONV, LRU_WIDTH), LRU_CONV ** -0.5),
        'lru_conv_b': nrm(ks[14], (DEPTH, LRU_WIDTH), 0.01),
        'lru_wa': nrm(ks[15], (DEPTH, LRU_BLOCKS, LRU_BLOCK, LRU_BLOCK), LRU_BLOCK ** -0.5),
        'lru_ba': nrm(ks[16], (DEPTH, LRU_WIDTH), 0.01),
        'lru_wx': nrm(ks[17], (DEPTH, LRU_BLOCKS, LRU_BLOCK, LRU_BLOCK), LRU_BLOCK ** -0.5),
        'lru_bx': nrm(ks[18], (DEPTH, LRU_WIDTH), 0.01),
        'lru_lambda': lru_lambda,
        'da_lam_q1': nrm(ks[19], (DEPTH, DA_DIM), 0.1),
        'da_lam_k1': nrm(ks[20], (DEPTH, DA_DIM), 0.1),
        'da_lam_q2': nrm(ks[21], (DEPTH, DA_DIM), 0.1),
        'da_lam_k2': nrm(ks[22], (DEPTH, DA_DIM), 0.1),
        'da_subln_g': 1.0 + nrm(ks[23], (DEPTH, DA_VDIM), 0.02),
        'w_br_a': nrm(ks[24], (DEPTH, SB_WIDTH, D_MODEL), SB_WIDTH ** -0.5),
        'w_br_b': nrm(ks[25], (DEPTH, LRU_WIDTH, D_MODEL), LRU_WIDTH ** -0.5),
        'w_br_c': nrm(ks[26], (DEPTH, DA_WIDTH, D_MODEL), DA_WIDTH ** -0.5),
        'w_out': nrm(ks[27], (DEPTH, D_MODEL, D_MODEL), d_scale),
        'norm2_g': 1.0 + nrm(ks[28], (DEPTH, D_MODEL), 0.02),
        'ffn_w_gate': nrm(ks[29], (DEPTH, D_MODEL, D_FF), d_scale),
        'ffn_w_up': nrm(ks[30], (DEPTH, D_MODEL, D_FF), d_scale),
        'ffn_conv_w': nrm(ks[31], (DEPTH, FFN_CONV, D_FF), FFN_CONV ** -0.5),
        'ffn_conv_b': nrm(ks[32], (DEPTH, D_FF), 0.01),
        'ffn_w_down': nrm(ks[33], (DEPTH, D_FF, D_MODEL), D_FF ** -0.5),
        'final_g': 1.0 + nrm(ks[34], (D_MODEL,), 0.02),
    }


def reference(x_prompt, x_sample, cache_sb_k, cache_sb_v, cache_da_k, cache_da_v,
              state_lru_h, state_lru_conv, state_ffn_conv, page_table,
              norm1_g, w_in, lru_conv_w, lru_conv_b, lru_wa, lru_ba, lru_wx, lru_bx, lru_lambda,
              da_lam_q1, da_lam_k1, da_lam_q2, da_lam_k2, da_subln_g,
              w_br_a, w_br_b, w_br_c, w_out, norm2_g,
              ffn_w_gate, ffn_w_up, ffn_conv_w, ffn_conv_b, ffn_w_down, final_g):
    pos_p = jnp.arange(x_prompt.shape[1])
    pos_s = PAST_LEN + jnp.arange(x_sample.shape[1])
    nb = x_sample.shape[0]
    xp, xs = x_prompt, x_sample
    st_p, st_s = [], []
    for l in range(DEPTH):
        lam_init = 0.8 - 0.6 * math.exp(-0.3 * l)
        p = dict(norm1_g=norm1_g[l], w_in=w_in[l],
                 lru_conv_w=lru_conv_w[l], lru_conv_b=lru_conv_b[l],
                 lru_wa=lru_wa[l], lru_ba=lru_ba[l], lru_wx=lru_wx[l], lru_bx=lru_bx[l],
                 lru_lambda=lru_lambda[l],
                 lam_q1=da_lam_q1[l], lam_k1=da_lam_k1[l], lam_q2=da_lam_q2[l], lam_k2=da_lam_k2[l],
                 subln_g=da_subln_g[l],
                 w_br_a=w_br_a[l], w_br_b=w_br_b[l], w_br_c=w_br_c[l], w_out=w_out[l],
                 norm2_g=norm2_g[l], ffn_w_gate=ffn_w_gate[l], ffn_w_up=ffn_w_up[l],
                 ffn_conv_w=ffn_conv_w[l], ffn_conv_b=ffn_conv_b[l], ffn_w_down=ffn_w_down[l])
        xp, new_p = _layer(xp, pos_p, p, lam_init, None)
        past = dict(
            sb_k=cache_sb_k[l, page_table].reshape(nb, -1, SB_HEADS, SB_DIM),
            sb_v=cache_sb_v[l, page_table].reshape(nb, -1, SB_HEADS, SB_DIM),
            da_k=cache_da_k[l, page_table].reshape(nb, -1, 2 * DA_HEADS, DA_DIM),
            da_v=cache_da_v[l, page_table].reshape(nb, -1, DA_HEADS, DA_VDIM),
            lru_h=state_lru_h[l], lru_conv=state_lru_conv[l], ffn_conv=state_ffn_conv[l])
        xs, new_s = _layer(xs, pos_s, p, lam_init, past)
        st_p.append(new_p)
        st_s.append(new_s)
    y_prompt = rmsnorm(xp, final_g)
    y_sample = rmsnorm(xs, final_g)
    p_sb_k, p_sb_v, p_da_k, p_da_v, p_lru_h, p_lru_conv, p_ffn_conv = [jnp.stack(z) for z in zip(*st_p)]
    s_sb_k, s_sb_v, s_da_k, s_da_v, s_lru_h, s_lru_conv, s_ffn_conv = [jnp.stack(z) for z in zip(*st_s)]
    return (y_prompt, y_sample,
            p_sb_k, p_sb_v, p_da_k, p_da_v, p_lru_h, p_lru_conv, p_ffn_conv,
            s_sb_k, s_sb_v, s_da_k, s_da_v, s_lru_h, s_lru_conv, s_ffn_conv)
```

```python
import functools
import math

import jax
import jax.numpy as jnp
from jax import lax
from jax.experimental import pallas as pl
from jax.experimental.pallas import tpu as pltpu

F32 = jnp.float32
BF16 = jnp.bfloat16

NORM_EPS = 1e-6
SUBLN_EPS = 1e-5
ROPE_THETA = 10000.0
LRU_C = 8.0
NEG_INF = -1e30
N_BRANCH = 3
LANES = 128
SUBLANES = 8
VMEM_LIMIT = 56 * 1024 * 1024
ATTN_TQ = 256
DA_GROUPS = 2


def _params(*sem):
    return pltpu.CompilerParams(dimension_semantics=sem, vmem_limit_bytes=VMEM_LIMIT)


def _tile(n, pref):
    if n <= pref:
        return n
    t = pref - pref % SUBLANES
    while t > SUBLANES and n % t:
        t -= SUBLANES
    assert n % t == 0, (n, pref)
    return t


def _dot(a, b):
    return jnp.dot(a, b, preferred_element_type=F32)


def _dot_nt(a, b):
    return lax.dot_general(a, b, (((1,), (1,)), ((), ())), preferred_element_type=F32)


def _rms(x, g, eps):
    y = x * lax.rsqrt(jnp.mean(x * x, axis=-1, keepdims=True) + eps)
    return y * g


def _softplus_neg_abs(z):
    return jnp.log(1.0 + jnp.exp(-jnp.abs(z)))


def _rmsnorm_kernel(x_ref, g_ref, o_ref, *, eps):
    o_ref[...] = _rms(x_ref[...], g_ref[...], eps).astype(o_ref.dtype)


def _rmsnorm(x, g, out_dtype):
    n, d = x.shape
    tm = _tile(n, 512)
    return pl.pallas_call(
        functools.partial(_rmsnorm_kernel, eps=NORM_EPS),
        grid=(n // tm,),
        in_specs=[pl.BlockSpec((tm, d), lambda i: (i, 0)),
                  pl.BlockSpec((1, d), lambda i: (0, 0))],
        out_specs=pl.BlockSpec((tm, d), lambda i: (i, 0)),
        out_shape=jax.ShapeDtypeStruct((n, d), out_dtype),
        compiler_params=_params("parallel"),
        name="rmsnorm",
    )(x, g.reshape(1, d))


def _store_q(q_ref, c, q, hd, expand):
    if not expand:
        q_ref[:, c * LANES:(c + 1) * LANES] = q.astype(q_ref.dtype)
        return
    lane = lax.broadcasted_iota(jnp.int32, q.shape, 1)
    for s in range(LANES // hd):
        keep = (lane // hd) == s
        lo = (c * (LANES // hd) + s) * LANES
        q_ref[:, lo:lo + LANES] = jnp.where(keep, q, 0.0).astype(q_ref.dtype)


def _proj_sb_kernel(xn_ref, w_ref, q_ref, k_ref, kb_ref, v_ref, vb_ref, lru_ref, *, wsb, hd, scale, expand):
    xn = xn_ref[...]
    q = _dot(xn, w_ref[:, 0:wsb]) * scale
    for c in range(wsb // LANES):
        _store_q(q_ref, c, q[:, c * LANES:(c + 1) * LANES], hd, expand)
    k = _dot(xn, w_ref[:, wsb:2 * wsb])
    k_ref[...] = k
    kb_ref[...] = k.astype(kb_ref.dtype)
    v = _dot(xn, w_ref[:, 2 * wsb:3 * wsb])
    v_ref[...] = v
    vb_ref[...] = v.astype(vb_ref.dtype)
    lru_ref[...] = _dot(xn, w_ref[:, 3 * wsb:])


def _proj_sb(xn, w, blk, wsb, wl, hd, expand):
    n, d = xn.shape
    tm = _tile(n, 512)
    bw = 3 * wsb + wl
    qw = wsb * (LANES // hd) if expand else wsb
    row = lambda width: pl.BlockSpec((tm, width), lambda i: (i, 0))
    return pl.pallas_call(
        functools.partial(_proj_sb_kernel, wsb=wsb, hd=hd, scale=hd ** -0.5, expand=expand),
        grid=(n // tm,),
        in_specs=[row(d), pl.BlockSpec((d, bw), lambda i: (0, blk))],
        out_specs=[row(qw), row(wsb), row(wsb), row(wsb), row(wsb), row(wl)],
        out_shape=[jax.ShapeDtypeStruct((n, qw), BF16),
                   jax.ShapeDtypeStruct((n, wsb), F32), jax.ShapeDtypeStruct((n, wsb), BF16),
                   jax.ShapeDtypeStruct((n, wsb), F32), jax.ShapeDtypeStruct((n, wsb), BF16),
                   jax.ShapeDtypeStruct((n, wl), F32)],
        compiler_params=_params("parallel"),
        name="proj_sb",
    )(xn, w)


def _rope_chunk(c, cos, sin_signed, first_half, half):
    swapped = jnp.where(first_half, pltpu.roll(c, LANES - half, axis=1), pltpu.roll(c, half, axis=1))
    return c * cos + swapped * sin_signed


def _proj_rope_kernel(xn_ref, w_ref, cos_ref, sin_ref, q_ref, k_ref, kb_ref, *, width, half, scale, expand):
    xn = xn_ref[...]
    cos = cos_ref[...]
    sin = sin_ref[...]
    lane = lax.broadcasted_iota(jnp.int32, cos.shape, 1)
    first_half = (lane % (2 * half)) < half
    for c in range(width // LANES):
        lo = c * LANES
        q = _rope_chunk(_dot(xn, w_ref[:, lo:lo + LANES]), cos, sin, first_half, half)
        _store_q(q_ref, c, q * scale, 2 * half, expand)
        k = _rope_chunk(_dot(xn, w_ref[:, width + lo:width + lo + LANES]), cos, sin, first_half, half)
        k_ref[:, lo:lo + LANES] = k
        kb_ref[:, lo:lo + LANES] = k.astype(kb_ref.dtype)


def _proj_rope(xn, w, blk, width, half, scale, cos, sin, rope_blocks, expand):
    n, d = xn.shape
    tm = _tile(n, 512)
    qw = width * (LANES // (2 * half)) if expand else width
    if rope_blocks:
        rope_map = lambda i: (i % rope_blocks, 0)
    else:
        rope_map = lambda i: (0, 0)
    row = lambda wd: pl.BlockSpec((tm, wd), lambda i: (i, 0))
    return pl.pallas_call(
        functools.partial(_proj_rope_kernel, width=width, half=half, scale=scale, expand=expand),
        grid=(n // tm,),
        in_specs=[row(d), pl.BlockSpec((d, 2 * width), lambda i: (0, blk)),
                  pl.BlockSpec((tm, LANES), rope_map), pl.BlockSpec((tm, LANES), rope_map)],
        out_specs=[row(qw), row(width), row(width)],
        out_shape=[jax.ShapeDtypeStruct((n, qw), BF16),
                   jax.ShapeDtypeStruct((n, width), F32), jax.ShapeDtypeStruct((n, width), BF16)],
        compiler_params=_params("parallel"),
        name="proj_rope",
    )(xn, w, cos, sin)


def _proj_plain_kernel(xn_ref, w_ref, o_ref, ob_ref):
    v = _dot(xn_ref[...], w_ref[...])
    o_ref[...] = v
    ob_ref[...] = v.astype(ob_ref.dtype)


def _proj_plain(xn, w, blk, width):
    n, d = xn.shape
    tm = _tile(n, 512)
    row = lambda wd: pl.BlockSpec((tm, wd), lambda i: (i, 0))
    return pl.pallas_call(
        _proj_plain_kernel,
        grid=(n // tm,),
        in_specs=[row(d), pl.BlockSpec((d, width), lambda i: (0, blk))],
        out_specs=[row(width), row(width)],
        out_shape=[jax.ShapeDtypeStruct((n, width), F32), jax.ShapeDtypeStruct((n, width), BF16)],
        compiler_params=_params("parallel"),
        name="proj_plain",
    )(xn, w)


def _proj_gate_kernel(xn_ref, w_ref, o_ref):
    o_ref[...] = jax.nn.sigmoid(_dot(xn_ref[...], w_ref[...]))


def _proj_gate(xn, w, blk0, bw, nblk):
    n, d = xn.shape
    tm = _tile(n, 512)
    return pl.pallas_call(
        _proj_gate_kernel,
        grid=(n // tm, nblk),
        in_specs=[pl.BlockSpec((tm, d), lambda i, j: (i, 0)),
                  pl.BlockSpec((d, bw), lambda i, j: (0, blk0 + j))],
        out_specs=pl.BlockSpec((tm, bw), lambda i, j: (i, j)),
        out_shape=jax.ShapeDtypeStruct((n, bw * nblk), F32),
        compiler_params=_params("parallel", "parallel"),
        name="proj_gate",
    )(xn, w)


def _sb_attn_kernel(q_ref, k_ref, v_ref, o_ref, *, tq, hd, pairs):
    qi = pl.program_id(2)
    per = LANES // hd
    row = lax.broadcasted_iota(jnp.int32, (tq, tq), 0)
    col = lax.broadcasted_iota(jnp.int32, (tq, tq), 1)
    causal = col < row
    after_mat = (row > col).astype(BF16)
    lane_head = lax.broadcasted_iota(jnp.int32, (tq, LANES), 1) // hd

    def block(j, carries, accs, masked):
        ks = pl.multiple_of(j * tq, tq)
        new_carries, new_accs = [], []
        for p in range(pairs):
            kp = k_ref[pl.ds(ks, tq), p * LANES:(p + 1) * LANES]
            vp = v_ref[pl.ds(ks, tq), p * LANES:(p + 1) * LANES]
            acc = accs[p]
            for s in range(per):
                h = p * per + s
                z = _dot_nt(q_ref[:, h * LANES:(h + 1) * LANES], kp)
                sp = _softplus_neg_abs(z)
                log_sig = jnp.minimum(z, 0.0) - sp
                log_keep = jnp.minimum(-z, 0.0) - sp
                if masked:
                    log_keep = jnp.where(causal, log_keep, 0.0)
                hi = log_keep.astype(BF16)
                lo = (log_keep - hi.astype(F32)).astype(BF16)
                after = _dot(hi, after_mat) + _dot(lo, after_mat) + carries[h]
                w = jnp.exp(log_sig + after)
                if masked:
                    w = jnp.where(causal, w, 0.0)
                acc = acc + jnp.where(lane_head == s, _dot(w.astype(BF16), vp), 0.0)
                new_carries.append(carries[h] + jnp.sum(log_keep, axis=1, keepdims=True))
            new_accs.append(acc)
        return tuple(new_carries), tuple(new_accs)

    carries = tuple(jnp.zeros((tq, 1), F32) for _ in range(pairs * per))
    accs = tuple(jnp.zeros((tq, LANES), F32) for _ in range(pairs))
    carries, accs = block(qi, carries, accs, True)
    carries, accs = lax.fori_loop(0, qi, lambda t, c: block(qi - 1 - t, c[0], c[1], False), (carries, accs))
    for p in range(pairs):
        o_ref[:, p * LANES:(p + 1) * LANES] = accs[p].astype(o_ref.dtype)


def _sb_attn(q, k, v, bsz, t, hd, pairs):
    n, width = k.shape
    per = LANES // hd
    tq = _tile(t, ATTN_TQ)
    nq = t // tq
    gw = pairs * LANES
    assert width % gw == 0 and q.shape[1] == width * per
    return pl.pallas_call(
        functools.partial(_sb_attn_kernel, tq=tq, hd=hd, pairs=pairs),
        grid=(bsz, width // gw, nq),
        in_specs=[pl.BlockSpec((tq, gw * per), lambda b, h, i: (b * nq + i, h)),
                  pl.BlockSpec((t, gw), lambda b, h, i: (b, h)),
                  pl.BlockSpec((t, gw), lambda b, h, i: (b, h))],
        out_specs=pl.BlockSpec((tq, gw), lambda b, h, i: (b * nq + i, h)),
        out_shape=jax.ShapeDtypeStruct((n, width), BF16),
        compiler_params=_params("parallel", "parallel", "parallel"),
        name="sb_attn",
    )(q, k, v)


def _da_lambda(lam_ref, lam_init):
    lv = lam_ref[...]
    s1 = jnp.sum(lv[0:1] * lv[1:2], axis=1, keepdims=True)
    s2 = jnp.sum(lv[2:3] * lv[3:4], axis=1, keepdims=True)
    return jnp.exp(s1) - jnp.exp(s2) + lam_init


def _da_attn_kernel(lam_ref, g_ref, q_ref, k_ref, v_ref, o_ref, *, tq, heads, lam_init):
    qi = pl.program_id(2)
    row = lax.broadcasted_iota(jnp.int32, (tq, tq), 0)
    col = lax.broadcasted_iota(jnp.int32, (tq, tq), 1)
    causal = col <= row

    def block(j, state, masked):
        ks = pl.multiple_of(j * tq, tq)
        out = []
        for h in range(heads):
            kp = k_ref[pl.ds(ks, tq), h * LANES:(h + 1) * LANES]
            vv = v_ref[pl.ds(ks, tq), h * LANES:(h + 1) * LANES]
            for s in range(2):
                m, l, acc = state[2 * h + s]
                sc = _dot_nt(q_ref[:, (2 * h + s) * LANES:(2 * h + s + 1) * LANES], kp)
                if masked:
                    sc = jnp.where(causal, sc, NEG_INF)
                m_new = jnp.maximum(m, jnp.max(sc, axis=1, keepdims=True))
                alpha = jnp.exp(m - m_new)
                p = jnp.exp(sc - m_new)
                l = alpha * l + jnp.sum(p, axis=1, keepdims=True)
                acc = alpha * acc + _dot(p.astype(BF16), vv)
                out.append((m_new, l, acc))
        return tuple(out)

    init = tuple((jnp.full((tq, 1), NEG_INF, F32), jnp.zeros((tq, 1), F32), jnp.zeros((tq, LANES), F32))
                 for _ in range(2 * heads))
    state = lax.fori_loop(0, qi, lambda j, st: block(j, st, False), init)
    state = block(qi, state, True)
    lam = _da_lambda(lam_ref, lam_init)
    for h in range(heads):
        (_, l1, a1), (_, l2, a2) = state[2 * h], state[2 * h + 1]
        o = a1 / l1 - lam * (a2 / l2)
        o_ref[:, h * LANES:(h + 1) * LANES] = (_rms(o, g_ref[...], SUBLN_EPS) * (1.0 - lam_init)).astype(o_ref.dtype)


def _da_attn(lam_vecs, g, q, k, v, bsz, t, lam_init, heads):
    n, width = k.shape
    tq = _tile(t, ATTN_TQ)
    nq = t // tq
    gw = heads * LANES
    assert width % gw == 0 and q.shape[1] == 2 * width and v.shape[1] == width
    return pl.pallas_call(
        functools.partial(_da_attn_kernel, tq=tq, heads=heads, lam_init=lam_init),
        grid=(bsz, width // gw, nq),
        in_specs=[pl.BlockSpec(lam_vecs.shape, lambda b, h, i: (0, 0)),
                  pl.BlockSpec((1, LANES), lambda b, h, i: (0, 0)),
                  pl.BlockSpec((tq, 2 * gw), lambda b, h, i: (b * nq + i, h)),
                  pl.BlockSpec((t, gw), lambda b, h, i: (b, h)),
                  pl.BlockSpec((t, gw), lambda b, h, i: (b, h))],
        out_specs=pl.BlockSpec((tq, gw), lambda b, h, i: (b * nq + i, h)),
        out_shape=jax.ShapeDtypeStruct((n, width), BF16),
        compiler_params=_params("parallel", "parallel", "parallel"),
        name="da_attn",
    )(lam_vecs, g, q, k, v)


def _lru_gates(xc, wa_ref, ba_ref, wx_ref, bx_ref, lam_ref):
    xb = xc.astype(BF16)
    r = jax.nn.sigmoid(_dot(xb, wa_ref[...]) + ba_ref[...])
    i = jax.nn.sigmoid(_dot(xb, wx_ref[...]) + bx_ref[...])
    lam = lam_ref[...]
    log_sig_lam = jnp.minimum(lam, 0.0) - _softplus_neg_abs(lam)
    log_a = LRU_C * r * log_sig_lam
    a = jnp.exp(log_a)
    u = jnp.sqrt(1.0 - jnp.exp(2.0 * log_a)) * i * xc
    return a, u


def _lru_kernel(x_ref, cw_ref, cb_ref, wa_ref, ba_ref, wx_ref, bx_ref, lam_ref,
                y_ref, hl_ref, xe_ref, h_ref, *, tc):
    @pl.when(pl.program_id(1) == 0)
    def _():
        xe_ref[0:SUBLANES, :] = jnp.zeros((SUBLANES, xe_ref.shape[1]), F32)
        h_ref[...] = jnp.zeros_like(h_ref)

    xe_ref[SUBLANES:SUBLANES + tc, :] = x_ref[...]
    cw = cw_ref[...]
    xc = cb_ref[...]
    for j in range(cw.shape[0]):
        off = SUBLANES - (cw.shape[0] - 1) + j
        xc = xc + xe_ref[off:off + tc, :] * cw[j:j + 1]
    xe_ref[0:SUBLANES, :] = xe_ref[tc:tc + SUBLANES, :]

    a, u = _lru_gates(xc, wa_ref, ba_ref, wx_ref, bx_ref, lam_ref)
    rowi = lax.broadcasted_iota(jnp.int32, a.shape, 0)
    s = 1
    while s < tc:
        keep = rowi >= s
        a_sh = jnp.where(keep, pltpu.roll(a, s, axis=0), 1.0)
        u_sh = jnp.where(keep, pltpu.roll(u, s, axis=0), 0.0)
        u = a * u_sh + u
        a = a * a_sh
        s *= 2
    h = a * h_ref[...] + u
    y_ref[...] = h.astype(y_ref.dtype)
    h_last = h[tc - 1:tc, :]
    h_ref[...] = h_last
    hl_ref[...] = h_last


def _lru(x, bsz, t, cw, cb, wa, ba, wx, bx, lam):
    n, w = x.shape
    tc = _tile(t, 256)
    nt = t // tc
    full = lambda a: pl.BlockSpec(a.shape, lambda b, i: (0,) * a.ndim)
    vec = lambda a: a.reshape(1, w)
    args = (cw, vec(cb), wa, vec(ba), wx, vec(bx), vec(lam))
    return pl.pallas_call(
        functools.partial(_lru_kernel, tc=tc),
        grid=(bsz, nt),
        in_specs=[pl.BlockSpec((tc, w), lambda b, i: (b * nt + i, 0))] + [full(a) for a in args],
        out_specs=[pl.BlockSpec((tc, w), lambda b, i: (b * nt + i, 0)),
                   pl.BlockSpec((None, 1, w), lambda b, i: (b, 0, 0))],
        out_shape=[jax.ShapeDtypeStruct((n, w), BF16), jax.ShapeDtypeStruct((bsz, 1, w), F32)],
        scratch_shapes=[pltpu.VMEM((tc + SUBLANES, w), F32), pltpu.VMEM((1, w), F32)],
        compiler_params=_params("arbitrary", "arbitrary"),
        name="lru",
    )(x, *args)


def _lru_step_kernel(x_ref, b0_ref, b1_ref, b2_ref, h0_ref, cw_ref, cb_ref, wa_ref, ba_ref, wx_ref, bx_ref,
                     lam_ref, y_ref, h_ref):
    cw = cw_ref[...]
    xc = cb_ref[...]
    for j, r in enumerate((b0_ref, b1_ref, b2_ref, x_ref)):
        xc = xc + r[...] * cw[j:j + 1]
    a, u = _lru_gates(xc, wa_ref, ba_ref, wx_ref, bx_ref, lam_ref)
    h = u + a * h0_ref[...]
    y_ref[...] = h.astype(y_ref.dtype)
    h_ref[...] = h


def _lru_step(x, bufs, h0, cw, cb, wa, ba, wx, bx, lam):
    m, w = x.shape
    vec = lambda a: a.reshape(1, w)
    return pl.pallas_call(
        _lru_step_kernel,
        out_shape=[jax.ShapeDtypeStruct((m, w), BF16), jax.ShapeDtypeStruct((m, w), F32)],
        compiler_params=pltpu.CompilerParams(vmem_limit_bytes=VMEM_LIMIT),
        name="lru_step",
    )(x, bufs[0], bufs[1], bufs[2], h0, cw, vec(cb), wa, vec(ba), wx, vec(bx), vec(lam))


def _merge_kernel(ya_ref, yb_ref, yc_ref, g_ref, wa_ref, wb_ref, wc_ref, o_ref, *, d):
    m = g_ref[:, 0:d] * _dot(ya_ref[...], wa_ref[...])
    m = m + g_ref[:, d:2 * d] * _dot(yb_ref[...], wb_ref[...])
    m = m + g_ref[:, 2 * d:3 * d] * _dot(yc_ref[...], wc_ref[...])
    o_ref[...] = m.astype(o_ref.dtype)


def _merge(ya, yb, yc, g, wa, wb, wc):
    n = ya.shape[0]
    d = wa.shape[1]
    tm = _tile(n, 256)
    row = lambda a: pl.BlockSpec((tm, a.shape[1]), lambda i: (i, 0))
    full = lambda a: pl.BlockSpec(a.shape, lambda i: (0, 0))
    return pl.pallas_call(
        functools.partial(_merge_kernel, d=d),
        grid=(n // tm,),
        in_specs=[row(ya), row(yb), row(yc), row(g), full(wa), full(wb), full(wc)],
        out_specs=pl.BlockSpec((tm, d), lambda i: (i, 0)),
        out_shape=jax.ShapeDtypeStruct((n, d), BF16),
        compiler_params=_params("parallel"),
        name="merge",
    )(ya, yb, yc, g, wa, wb, wc)


def _outproj_kernel(m_ref, w_ref, x_ref, g_ref, xo_ref, xn_ref):
    xo = x_ref[...] + _dot(m_ref[...], w_ref[...])
    xo_ref[...] = xo
    xn_ref[...] = _rms(xo, g_ref[...], NORM_EPS).astype(xn_ref.dtype)


def _outproj(m, w, x, g):
    n, d = x.shape
    tm = _tile(n, 512)
    row = pl.BlockSpec((tm, d), lambda i: (i, 0))
    return pl.pallas_call(
        _outproj_kernel,
        grid=(n // tm,),
        in_specs=[row, pl.BlockSpec(w.shape, lambda i: (0, 0)), row, pl.BlockSpec((1, d), lambda i: (0, 0))],
        out_specs=[row, row],
        out_shape=[jax.ShapeDtypeStruct((n, d), F32), jax.ShapeDtypeStruct((n, d), BF16)],
        compiler_params=_params("parallel"),
        name="outproj",
    )(m, w, x, g.reshape(1, d))


def _ffn_epilogue(j, nj, part, x_ref, gn_ref, out_refs, acc_ref):
    @pl.when(j == 0)
    def _():
        acc_ref[...] = part

    @pl.when(j > 0)
    def _():
        acc_ref[...] += part

    @pl.when(j == nj - 1)
    def _():
        xo = x_ref[...] + acc_ref[...]
        if len(out_refs) == 2:
            out_refs[0][...] = xo
        out_refs[-1][...] = _rms(xo, gn_ref[...], NORM_EPS).astype(out_refs[-1].dtype)


def _ffn_kernel(xn_ref, wg_ref, wu_ref, wd_ref, cw_ref, cb_ref, x_ref, gn_ref, *rest,
                tm, tiles_per_seq, nj, n_out):
    out_refs = rest[:n_out]
    tail_ref, acc_ref, ge_ref, carry_ref = rest[n_out:]
    i = pl.program_id(0)
    j = pl.program_id(1)
    xn = xn_ref[...]
    gp = _dot(xn, wg_ref[...])
    seq_start = (i % tiles_per_seq) == 0

    @pl.when(seq_start)
    def _():
        ge_ref[0:SUBLANES, :] = jnp.zeros((SUBLANES, ge_ref.shape[1]), F32)

    @pl.when(jnp.logical_not(seq_start))
    def _():
        ge_ref[0:SUBLANES, :] = carry_ref[j]

    ge_ref[SUBLANES:SUBLANES + tm, :] = gp
    tail = ge_ref[tm:tm + SUBLANES, :]
    carry_ref[j] = tail
    tail_ref[...] = tail
    cw = cw_ref[...]
    g = cb_ref[...]
    for c in range(cw.shape[0]):
        off = SUBLANES - (cw.shape[0] - 1) + c
        g = g + ge_ref[off:off + tm, :] * cw[c:c + 1]
    act = (g * jax.nn.sigmoid(g) * _dot(xn, wu_ref[...])).astype(BF16)
    _ffn_epilogue(j, nj, _dot(act, wd_ref[...]), x_ref, gn_ref, out_refs, acc_ref)


def _ffn(xn, wg, wu, wd, cw, cb, x, g_next, bsz, t, emit_x, norm_dtype):
    n, d = x.shape
    f = wg.shape[1]
    tm = _tile(t, 512)
    tf = _tile(f, 512)
    nj = f // tf
    tiles_per_seq = t // tm
    row = pl.BlockSpec((tm, d), lambda i, j: (i, 0))
    out_specs = ([row] if emit_x else []) + [row]
    out_shape = ([jax.ShapeDtypeStruct((n, d), F32)] if emit_x else []) + [jax.ShapeDtypeStruct((n, d), norm_dtype)]
    out_specs.append(pl.BlockSpec((None, SUBLANES, tf), lambda i, j: (i, 0, j)))
    out_shape.append(jax.ShapeDtypeStruct((n // tm, SUBLANES, f), F32))
    return pl.pallas_call(
        functools.partial(_ffn_kernel, tm=tm, tiles_per_seq=tiles_per_seq, nj=nj, n_out=len(out_specs) - 1),
        grid=(n // tm, nj),
        in_specs=[row,
                  pl.BlockSpec((d, tf), lambda i, j: (0, j)), pl.BlockSpec((d, tf), lambda i, j: (0, j)),
                  pl.BlockSpec((tf, d), lambda i, j: (j, 0)),
                  pl.BlockSpec((cw.shape[0], tf), lambda i, j: (0, j)), pl.BlockSpec((1, tf), lambda i, j: (0, j)),
                  row, pl.BlockSpec((1, d), lambda i, j: (0, 0))],
        out_specs=out_specs,
        out_shape=out_shape,
        scratch_shapes=[pltpu.VMEM((tm, d), F32), pltpu.VMEM((tm + SUBLANES, tf), F32),
                        pltpu.VMEM((nj, SUBLANES, tf), F32)],
        compiler_params=_params("arbitrary", "arbitrary"),
        name="ffn",
    )(xn, wg, wu, wd, cw, cb.reshape(1, f), x, g_next.reshape(1, d))


def _ffn_step_kernel(xn_ref, wg_ref, wu_ref, wd_ref, cw_ref, cb_ref, s0_ref, s1_ref, x_ref, gn_ref, *rest,
                     nj, n_out):
    out_refs = rest[:n_out]
    gp_ref, acc_ref = rest[n_out:]
    j = pl.program_id(0)
    xn = xn_ref[...]
    gp = _dot(xn, wg_ref[...])
    gp_ref[...] = gp
    cw = cw_ref[...]
    g = cb_ref[...]
    for c, r in enumerate((s0_ref[...], s1_ref[...], gp)):
        g = g + r * cw[c:c + 1]
    act = (g * jax.nn.sigmoid(g) * _dot(xn, wu_ref[...])).astype(BF16)
    _ffn_epilogue(j, nj, _dot(act, wd_ref[...]), x_ref, gn_ref, out_refs, acc_ref)


def _ffn_step(xn, wg, wu, wd, cw, cb, s0, s1, x, g_next, emit_x, norm_dtype):
    m, d = x.shape
    f = wg.shape[1]
    tf = _tile(f, 512)
    nj = f // tf
    row = pl.BlockSpec((m, d), lambda j: (0, 0))
    col = pl.BlockSpec((m, tf), lambda j: (0, j))
    out_specs = ([row] if emit_x else []) + [row, col]
    out_shape = (([jax.ShapeDtypeStruct((m, d), F32)] if emit_x else [])
                 + [jax.ShapeDtypeStruct((m, d), norm_dtype), jax.ShapeDtypeStruct((m, f), F32)])
    return pl.pallas_call(
        functools.partial(_ffn_step_kernel, nj=nj, n_out=len(out_specs) - 1),
        grid=(nj,),
        in_specs=[row,
                  pl.BlockSpec((d, tf), lambda j: (0, j)), pl.BlockSpec((d, tf), lambda j: (0, j)),
                  pl.BlockSpec((tf, d), lambda j: (j, 0)),
                  pl.BlockSpec((cw.shape[0], tf), lambda j: (0, j)), pl.BlockSpec((1, tf), lambda j: (0, j)),
                  col, col, row, pl.BlockSpec((1, d), lambda j: (0, 0))],
        out_specs=out_specs,
        out_shape=out_shape,
        scratch_shapes=[pltpu.VMEM((m, d), F32)],
        compiler_params=_params("arbitrary"),
        name="ffn_step",
    )(xn, wg, wu, wd, cw, cb.reshape(1, f), s0, s1, x, g_next.reshape(1, d))


def _head_select(rows, width, group):
    sub = lax.broadcasted_iota(jnp.int32, (rows, width), 0)
    lane = lax.broadcasted_iota(jnp.int32, (rows, width), 1)
    return (lane // group) == sub


def _block_diag_q(q_ref, rows, hd):
    width = q_ref.shape[-1]
    sel = _head_select(rows, width, hd)
    return jnp.where(sel, jnp.broadcast_to(q_ref[...].astype(F32), (rows, width)), 0.0).astype(BF16)


def _dec_sb_kernel(pt_ref, q_ref, *rest, npg, hd, rows):
    k_refs = rest[:npg]
    v_refs = rest[npg:2 * npg]
    o_ref, carry_ref, acc_ref = rest[2 * npg:]
    g = pl.program_id(1)
    width = q_ref.shape[-1]
    page = k_refs[0].shape[-1]

    @pl.when(g == 0)
    def _():
        carry_ref[...] = jnp.zeros_like(carry_ref)
        acc_ref[...] = jnp.zeros_like(acc_ref)

    qbd = _block_diag_q(q_ref, rows, hd)
    order = list(reversed(range(npg)))
    z = jnp.concatenate([_dot(qbd, k_refs[p][...].astype(BF16)) for p in order], axis=0)
    sp = _softplus_neg_abs(z)
    log_sig = jnp.minimum(z, 0.0) - sp
    log_keep = jnp.minimum(-z, 0.0) - sp
    hi = log_keep.astype(BF16)
    lo = (log_keep - hi.astype(F32)).astype(BF16)
    r = lax.broadcasted_iota(jnp.int32, (page, 2 * page), 0)
    c = lax.broadcasted_iota(jnp.int32, (page, 2 * page), 1)
    sum_mat = ((r > c) | (c >= page)).astype(BF16)
    sums = _dot(hi, sum_mat) + _dot(lo, sum_mat)
    off = carry_ref[...]
    offs = []
    for i in range(npg):
        offs.append(off)
        off = off + sums[i * rows:(i + 1) * rows, page:]
    carry_ref[...] = off
    w = jnp.exp(log_sig + sums[:, :page] + jnp.concatenate(offs, axis=0)).astype(BF16)
    acc = acc_ref[...]
    for i, p in enumerate(order):
        acc = acc + _dot_nt(w[i * rows:(i + 1) * rows], v_refs[p][...].astype(BF16))
    acc_ref[...] = acc

    @pl.when(g == pl.num_programs(1) - 1)
    def _():
        sel = _head_select(rows, width, hd)
        o_ref[...] = jnp.sum(jnp.where(sel, acc, 0.0), axis=0, keepdims=True).astype(o_ref.dtype)


def _page_specs(layer, npg, ngroups, block, newest_first):
    specs = []
    for p in range(npg):
        if newest_first:
            imap = lambda b, g, pt, p=p: (layer, pt[b, (ngroups - 1 - g) * npg + p], 0, 0)
        else:
            imap = lambda b, g, pt, p=p: (layer, pt[b, g * npg + p], 0, 0)
        specs.append(pl.BlockSpec((None, None) + block, imap))
    return specs


def _pages_per_step(n_pages, pref):
    npg = min(n_pages, pref)
    while n_pages % npg:
        npg -= 1
    return npg


def _dec_sb(page_table, q, cache_kt, cache_vt, layer, hd):
    m, width = q.shape
    n_pages = page_table.shape[1]
    page = cache_kt.shape[3]
    assert page == LANES and cache_kt.shape[2] == width
    npg = _pages_per_step(n_pages, 16)
    ngroups = n_pages // npg
    rows = 2 * SUBLANES
    assert width // hd <= rows
    vec = pl.BlockSpec((None, 1, width), lambda b, g, pt: (b, 0, 0))
    grid_spec = pltpu.PrefetchScalarGridSpec(
        num_scalar_prefetch=1,
        grid=(m, ngroups),
        in_specs=[vec] + _page_specs(layer, npg, ngroups, (width, page), True)
                 + _page_specs(layer, npg, ngroups, (width, page), True),
        out_specs=vec,
        scratch_shapes=[pltpu.VMEM((rows, page), F32), pltpu.VMEM((rows, width), F32)],
    )
    out = pl.pallas_call(
        functools.partial(_dec_sb_kernel, npg=npg, hd=hd, rows=rows),
        grid_spec=grid_spec,
        out_shape=jax.ShapeDtypeStruct((m, 1, width), BF16),
        compiler_params=_params("parallel", "arbitrary"),
        name="dec_sb",
    )(page_table, q.reshape(m, 1, width), *([cache_kt] * npg), *([cache_vt] * npg))
    return out.reshape(m, width)


def _dec_da_kernel(pt_ref, lam_ref, g_ref, q_ref, kn_ref, vn_ref, *rest, npg, hd, heads, rows, lam_init):
    k_refs = rest[:npg]
    v_refs = rest[npg:2 * npg]
    o_ref, m_ref, l_ref, acc_ref = rest[2 * npg:]
    g = pl.program_id(1)
    page = k_refs[0].shape[-1]
    vd = acc_ref.shape[-1]
    sub = lax.broadcasted_iota(jnp.int32, (rows, vd), 0)
    qbd = _block_diag_q(q_ref, rows, hd)

    @pl.when(g == 0)
    def _():
        kn = jnp.broadcast_to(kn_ref[...], qbd.shape).astype(F32)
        m_ref[...] = jnp.sum(qbd.astype(F32) * kn, axis=1, keepdims=True)
        l_ref[...] = jnp.ones_like(l_ref)
        vn = vn_ref[...].astype(F32)
        acc0 = jnp.zeros((rows, vd), F32)
        for h in range(heads):
            acc0 = jnp.where(sub // 2 == h, jnp.broadcast_to(vn[h:h + 1], (rows, vd)), acc0)
        acc_ref[...] = acc0

    sc = jnp.concatenate([_dot(qbd, k_refs[p][...].astype(BF16)) for p in range(npg)], axis=1)
    m_old = m_ref[...]
    m_new = jnp.maximum(m_old, jnp.max(sc, axis=1, keepdims=True))
    alpha = jnp.exp(m_old - m_new)
    pw = jnp.exp(sc - m_new)
    m_ref[...] = m_new
    l_ref[...] = alpha * l_ref[...] + jnp.sum(pw, axis=1, keepdims=True)
    acc = alpha * acc_ref[...]
    for p in range(npg):
        pp = pw[:, p * page:(p + 1) * page]
        for h in range(heads):
            vh = v_refs[p][pl.ds(h, page, stride=heads), :].astype(BF16)
            acc = acc + _dot(jnp.where(sub // 2 == h, pp, 0.0).astype(BF16), vh)
    acc_ref[...] = acc

    @pl.when(g == pl.num_programs(1) - 1)
    def _():
        lam = _da_lambda(lam_ref, lam_init)
        accn = acc / l_ref[...]
        gain = g_ref[...]
        for h in range(heads):
            oh = accn[2 * h:2 * h + 1] - lam * accn[2 * h + 1:2 * h + 2]
            o_ref[:, h * vd:(h + 1) * vd] = (_rms(oh, gain, SUBLN_EPS) * (1.0 - lam_init)).astype(o_ref.dtype)


def _dec_da(page_table, lam_vecs, gain, q, k_new, v_new, cache_kt, cache_v, layer, hd, heads, lam_init):
    m, kwidth = q.shape
    vd = cache_v.shape[3]
    n_pages = page_table.shape[1]
    page = cache_kt.shape[3]
    assert page == LANES and vd == LANES and cache_kt.shape[2] == kwidth and cache_v.shape[2] == page * heads
    npg = _pages_per_step(n_pages, 8)
    ngroups = n_pages // npg
    rows = -(-(kwidth // hd) // (2 * SUBLANES)) * (2 * SUBLANES)
    kvec = pl.BlockSpec((None, 1, kwidth), lambda b, g, pt: (b, 0, 0))
    grid_spec = pltpu.PrefetchScalarGridSpec(
        num_scalar_prefetch=1,
        grid=(m, ngroups),
        in_specs=[pl.BlockSpec(lam_vecs.shape, lambda b, g, pt: (0, 0)),
                  pl.BlockSpec((1, vd), lambda b, g, pt: (0, 0)),
                  kvec, kvec, pl.BlockSpec((None, heads, vd), lambda b, g, pt: (b, 0, 0))]
                 + _page_specs(layer, npg, ngroups, (kwidth, page), False)
                 + _page_specs(layer, npg, ngroups, (page * heads, vd), False),
        out_specs=pl.BlockSpec((None, 1, heads * vd), lambda b, g, pt: (b, 0, 0)),
        scratch_shapes=[pltpu.VMEM((rows, 1), F32), pltpu.VMEM((rows, 1), F32), pltpu.VMEM((rows, vd), F32)],
    )
    out = pl.pallas_call(
        functools.partial(_dec_da_kernel, npg=npg, hd=hd, heads=heads, rows=rows, lam_init=lam_init),
        grid_spec=grid_spec,
        out_shape=jax.ShapeDtypeStruct((m, 1, heads * vd), BF16),
        compiler_params=_params("parallel", "arbitrary"),
        name="dec_da",
    )(page_table, lam_vecs, gain, q.reshape(m, 1, kwidth), k_new.reshape(m, 1, kwidth),
      v_new.reshape(m, heads, vd), *([cache_kt] * npg), *([cache_v] * npg))
    return out.reshape(m, heads * vd)


def _rope_tables(pos, hd):
    half = hd // 2
    inv_freq = ROPE_THETA ** (-jnp.arange(half, dtype=F32) / half)
    ang = pos.astype(F32)[:, None] * inv_freq[None, :]
    cos = jnp.cos(ang)
    sin = jnp.sin(ang)
    reps = LANES // hd
    cos_t = jnp.tile(jnp.concatenate([cos, cos], axis=-1), (1, reps))
    sin_t = jnp.tile(jnp.concatenate([-sin, sin], axis=-1), (1, reps))
    return cos_t, sin_t


def _block_diag(w):
    nb, bi, bo = w.shape
    eye = jnp.eye(nb, dtype=w.dtype)
    return (eye[:, None, :, None] * w[:, :, None, :]).reshape(nb * bi, nb * bo)


def kernel(x_prompt, x_sample, cache_sb_k, cache_sb_v, cache_da_k, cache_da_v, state_lru_h, state_lru_conv, state_ffn_conv, page_table, norm1_g, w_in, lru_conv_w, lru_conv_b, lru_wa, lru_ba, lru_wx, lru_bx, lru_lambda, da_lam_q1, da_lam_k1, da_lam_q2, da_lam_k2, da_subln_g, w_br_a, w_br_b, w_br_c, w_out, norm2_g, ffn_w_gate, ffn_w_up, ffn_conv_w, ffn_conv_b, ffn_w_down, final_g):
    bsz, t, d = x_prompt.shape
    m = x_sample.shape[0]
    assert x_sample.shape[1] == 1
    depth, n_pool, page, sb_heads, sb_dim = cache_sb_k.shape
    da_sub, da_dim = cache_da_k.shape[3:]
    da_heads, da_vdim = cache_da_v.shape[3:]
    sbw = sb_heads * sb_dim
    lw = state_lru_h.shape[-1]
    daw = da_heads * da_vdim
    assert da_sub * da_dim == daw and da_vdim == LANES and LANES % sb_dim == 0 and LANES % da_dim == 0
    f = ffn_w_gate.shape[-1]
    past_len = page_table.shape[1] * page
    n = bsz * t

    blk_sb = 0
    off_da = 3 * sbw + lw
    assert off_da % (2 * daw) == 0 and (off_da + 2 * daw) % daw == 0 and (off_da + 3 * daw) % daw == 0
    blk_rope = off_da // (2 * daw)
    blk_dav = (off_da + 2 * daw) // daw
    gate_bw = daw
    assert (N_BRANCH * d) % gate_bw == 0
    blk_gate = (off_da + 3 * daw) // gate_bw
    n_gate = N_BRANCH * d // gate_bw

    cos_p, sin_p = _rope_tables(jnp.arange(t), da_dim)
    cos_s, sin_s = _rope_tables(jnp.full((m,), past_len), da_dim)
    tm_p = _tile(n, 512)
    assert t % tm_p == 0
    rope_blocks_p = t // tm_p

    to_kt = lambda c, w: jnp.transpose(c, (0, 1, 3, 4, 2)).reshape(depth, n_pool, w, page)
    ck_sb = to_kt(cache_sb_k, sbw)
    cv_sb = to_kt(cache_sb_v, sbw)
    ck_da = to_kt(cache_da_k, daw)
    cv_da = cache_da_v.reshape(depth, n_pool, page * da_heads, da_vdim)

    xp = x_prompt.reshape(n, d)
    xs = x_sample.reshape(m, d)
    xnp = _rmsnorm(xp, norm1_g[0], BF16)
    xns = _rmsnorm(xs, norm1_g[0], BF16)
    st_p, st_s = [], []
    for l in range(depth):
        lam_init = 0.8 - 0.6 * math.exp(-0.3 * l)
        last = l == depth - 1
        w_in_b = w_in[l].astype(BF16)
        wa_bd = _block_diag(lru_wa[l]).astype(BF16)
        wx_bd = _block_diag(lru_wx[l]).astype(BF16)
        wbr_a = w_br_a[l].astype(BF16)
        wbr_b = w_br_b[l].astype(BF16)
        wbr_c = w_br_c[l].astype(BF16)
        wo = w_out[l].astype(BF16)
        wg = ffn_w_gate[l].astype(BF16)
        wu = ffn_w_up[l].astype(BF16)
        wd = ffn_w_down[l].astype(BF16)
        lam_vecs = jnp.stack([da_lam_q1[l], da_lam_k1[l], da_lam_q2[l], da_lam_k2[l]])
        gain = da_subln_g[l].reshape(1, da_vdim)
        g_next = final_g if last else norm1_g[l + 1]
        norm_dtype = F32 if last else BF16
        lru_args = (lru_conv_w[l], lru_conv_b[l], wa_bd, lru_ba[l], wx_bd, lru_bx[l], lru_lambda[l])

        sbq, sbk, sbk_b, sbv, sbv_b, lru_in = _proj_sb(xnp, w_in_b, blk_sb, sbw, lw, sb_dim, True)
        daq, dak, dak_b = _proj_rope(xnp, w_in_b, blk_rope, daw, da_dim // 2, da_dim ** -0.5,
                                     cos_p, sin_p, rope_blocks_p, True)
        dav, dav_b = _proj_plain(xnp, w_in_b, blk_dav, daw)
        gates = _proj_gate(xnp, w_in_b, blk_gate, gate_bw, n_gate)
        y_sb = _sb_attn(sbq, sbk_b, sbv_b, bsz, t, sb_dim, sbw // LANES)
        y_da = _da_attn(lam_vecs, gain, daq, dak_b, dav_b, bsz, t, lam_init, da_heads // DA_GROUPS)
        y_lru, h_last = _lru(lru_in, bsz, t, *lru_args)
        merged = _merge(y_sb, y_lru, y_da, gates, wbr_a, wbr_b, wbr_c)
        xp, xn2 = _outproj(merged, wo, xp, norm2_g[l])
        outs = _ffn(xn2, wg, wu, wd, ffn_conv_w[l], ffn_conv_b[l], xp, g_next, bsz, t, not last, norm_dtype)
        if last:
            y_prompt, tail = outs
        else:
            xp, xnp, tail = outs
        kconv = lru_conv_w.shape[1] - 1
        kffn = ffn_conv_w.shape[1] - 1
        st_p.append((sbk.reshape(bsz, t, sb_heads, sb_dim), sbv.reshape(bsz, t, sb_heads, sb_dim),
                     dak.reshape(bsz, t, da_sub, da_dim), dav.reshape(bsz, t, da_heads, da_vdim),
                     h_last.reshape(bsz, lw), lru_in.reshape(bsz, t, lw)[:, t - kconv:],
                     tail[t // _tile(t, 512) - 1::t // _tile(t, 512), SUBLANES - kffn:]))

        sbq, sbk, _, sbv, _, lru_in = _proj_sb(xns, w_in_b, blk_sb, sbw, lw, sb_dim, False)
        daq, dak, dak_b = _proj_rope(xns, w_in_b, blk_rope, daw, da_dim // 2, da_dim ** -0.5, cos_s, sin_s, 0, False)
        dav, dav_b = _proj_plain(xns, w_in_b, blk_dav, daw)
        gates = _proj_gate(xns, w_in_b, blk_gate, gate_bw, n_gate)
        y_sb = _dec_sb(page_table, sbq, ck_sb, cv_sb, l, sb_dim)
        y_da = _dec_da(page_table, lam_vecs, gain, daq, dak_b, dav_b, ck_da, cv_da, l, da_dim, da_heads, lam_init)
        conv_state = state_lru_conv[l]
        y_lru, h_new = _lru_step(lru_in, [conv_state[:, c] for c in range(kconv)], state_lru_h[l], *lru_args)
        merged = _merge(y_sb, y_lru, y_da, gates, wbr_a, wbr_b, wbr_c)
        xs, xn2 = _outproj(merged, wo, xs, norm2_g[l])
        ffn_state = state_ffn_conv[l]
        outs = _ffn_step(xn2, wg, wu, wd, ffn_conv_w[l], ffn_conv_b[l], ffn_state[:, 0], ffn_state[:, 1],
                         xs, g_next, not last, norm_dtype)
        if last:
            y_sample, gp = outs
        else:
            xs, xns, gp = outs
        st_s.append((sbk.reshape(m, 1, sb_heads, sb_dim), sbv.reshape(m, 1, sb_heads, sb_dim),
                     dak.reshape(m, 1, da_sub, da_dim), dav.reshape(m, 1, da_heads, da_vdim),
                     h_new, jnp.concatenate([conv_state[:, 1:], lru_in[:, None]], axis=1),
                     jnp.concatenate([ffn_state[:, 1:], gp[:, None]], axis=1)))

    p_state = [jnp.stack(z) for z in zip(*st_p)]
    s_state = [jnp.stack(z) for z in zip(*st_s)]
    return (y_prompt.reshape(bsz, t, d), y_sample.reshape(m, 1, d), *p_state, *s_state)
```

```python
import functools
import math

import jax
import jax.numpy as jnp
from jax import lax
from jax.experimental import pallas as pl
from jax.experimental.pallas import tpu as pltpu

F32 = jnp.float32
BF16 = jnp.bfloat16

NORM_EPS = 1e-6
SUBLN_EPS = 1e-5
ROPE_THETA = 10000.0
LRU_C = 8.0
NEG_INF = -1e30
N_BRANCH = 3
LANES = 128
SUBLANES = 8
VMEM_LIMIT = 56 * 1024 * 1024
ATTN_TQ = 512
DA_GROUPS = 4
FFN_CHUNK = 256


def _params(*sem):
    return pltpu.CompilerParams(dimension_semantics=sem, vmem_limit_bytes=VMEM_LIMIT)


def _tile(n, pref):
    if n <= pref:
        return n
    t = pref - pref % SUBLANES
    while t > SUBLANES and n % t:
        t -= SUBLANES
    assert n % t == 0, (n, pref)
    return t


def _dot(a, b):
    return jnp.dot(a, b, preferred_element_type=F32)


def _dot_nt(a, b):
    return lax.dot_general(a, b, (((1,), (1,)), ((), ())), preferred_element_type=F32)


def _rms(x, g, eps):
    y = x * lax.rsqrt(jnp.mean(x * x, axis=-1, keepdims=True) + eps)
    return y * g


def _softplus_neg_abs(z):
    return jnp.log(1.0 + jnp.exp(-jnp.abs(z)))


def _rmsnorm_kernel(x_ref, g_ref, o_ref, *, eps):
    o_ref[...] = _rms(x_ref[...], g_ref[...], eps).astype(o_ref.dtype)


def _rmsnorm(x, g, out_dtype):
    n, d = x.shape
    tm = _tile(n, 512)
    return pl.pallas_call(
        functools.partial(_rmsnorm_kernel, eps=NORM_EPS),
        grid=(n // tm,),
        in_specs=[pl.BlockSpec((tm, d), lambda i: (i, 0)),
                  pl.BlockSpec((1, d), lambda i: (0, 0))],
        out_specs=pl.BlockSpec((tm, d), lambda i: (i, 0)),
        out_shape=jax.ShapeDtypeStruct((n, d), out_dtype),
        compiler_params=_params("parallel"),
        name="rmsnorm",
    )(x, g.reshape(1, d))


def _store_q(q_ref, c, q, hd, expand):
    if not expand:
        q_ref[:, c * LANES:(c + 1) * LANES] = q.astype(q_ref.dtype)
        return
    lane = lax.broadcasted_iota(jnp.int32, q.shape, 1)
    for s in range(LANES // hd):
        keep = (lane // hd) == s
        lo = (c * (LANES // hd) + s) * LANES
        q_ref[:, lo:lo + LANES] = jnp.where(keep, q, 0.0).astype(q_ref.dtype)


def _proj_sb_kernel(xn_ref, w_ref, q_ref, k_ref, kb_ref, v_ref, vb_ref, lru_ref, *, wsb, hd, scale, expand):
    xn = xn_ref[...]
    q = _dot(xn, w_ref[:, 0:wsb]) * scale
    for c in range(wsb // LANES):
        _store_q(q_ref, c, q[:, c * LANES:(c + 1) * LANES], hd, expand)
    k = _dot(xn, w_ref[:, wsb:2 * wsb])
    k_ref[...] = k
    kb_ref[...] = k.astype(kb_ref.dtype)
    v = _dot(xn, w_ref[:, 2 * wsb:3 * wsb])
    v_ref[...] = v
    vb_ref[...] = v.astype(vb_ref.dtype)
    lru_ref[...] = _dot(xn, w_ref[:, 3 * wsb:])


def _proj_sb(xn, w, blk, wsb, wl, hd, expand):
    n, d = xn.shape
    tm = _tile(n, 512)
    bw = 3 * wsb + wl
    qw = wsb * (LANES // hd) if expand else wsb
    row = lambda width: pl.BlockSpec((tm, width), lambda i: (i, 0))
    return pl.pallas_call(
        functools.partial(_proj_sb_kernel, wsb=wsb, hd=hd, scale=hd ** -0.5, expand=expand),
        grid=(n // tm,),
        in_specs=[row(d), pl.BlockSpec((d, bw), lambda i: (0, blk))],
        out_specs=[row(qw), row(wsb), row(wsb), row(wsb), row(wsb), row(wl)],
        out_shape=[jax.ShapeDtypeStruct((n, qw), BF16),
                   jax.ShapeDtypeStruct((n, wsb), F32), jax.ShapeDtypeStruct((n, wsb), BF16),
                   jax.ShapeDtypeStruct((n, wsb), F32), jax.ShapeDtypeStruct((n, wsb), BF16),
                   jax.ShapeDtypeStruct((n, wl), F32)],
        compiler_params=_params("parallel"),
        name="proj_sb",
    )(xn, w)


def _rope_chunk(c, cos, sin_signed, first_half, half):
    swapped = jnp.where(first_half, pltpu.roll(c, LANES - half, axis=1), pltpu.roll(c, half, axis=1))
    return c * cos + swapped * sin_signed


def _proj_rope_kernel(xn_ref, w_ref, cos_ref, sin_ref, q_ref, k_ref, kb_ref, *, width, half, scale, expand):
    xn = xn_ref[...]
    cos = cos_ref[...]
    sin = sin_ref[...]
    lane = lax.broadcasted_iota(jnp.int32, cos.shape, 1)
    first_half = (lane % (2 * half)) < half
    step = 2 * LANES if width % (2 * LANES) == 0 else LANES
    for c0 in range(0, width, step):
        qs = _dot(xn, w_ref[:, c0:c0 + step])
        ks = _dot(xn, w_ref[:, width + c0:width + c0 + step])
        for o in range(0, step, LANES):
            lo = c0 + o
            q = _rope_chunk(qs[:, o:o + LANES], cos, sin, first_half, half)
            _store_q(q_ref, lo // LANES, q * scale, 2 * half, expand)
            k = _rope_chunk(ks[:, o:o + LANES], cos, sin, first_half, half)
            k_ref[:, lo:lo + LANES] = k
            kb_ref[:, lo:lo + LANES] = k.astype(kb_ref.dtype)


def _proj_rope(xn, w, blk, width, half, scale, cos, sin, rope_blocks, expand):
    n, d = xn.shape
    tm = _tile(n, 512)
    qw = width * (LANES // (2 * half)) if expand else width
    if rope_blocks:
        rope_map = lambda i: (i % rope_blocks, 0)
    else:
        rope_map = lambda i: (0, 0)
    row = lambda wd: pl.BlockSpec((tm, wd), lambda i: (i, 0))
    return pl.pallas_call(
        functools.partial(_proj_rope_kernel, width=width, half=half, scale=scale, expand=expand),
        grid=(n // tm,),
        in_specs=[row(d), pl.BlockSpec((d, 2 * width), lambda i: (0, blk)),
                  pl.BlockSpec((tm, LANES), rope_map), pl.BlockSpec((tm, LANES), rope_map)],
        out_specs=[row(qw), row(width), row(width)],
        out_shape=[jax.ShapeDtypeStruct((n, qw), BF16),
                   jax.ShapeDtypeStruct((n, width), F32), jax.ShapeDtypeStruct((n, width), BF16)],
        compiler_params=_params("parallel"),
        name="proj_rope",
    )(xn, w, cos, sin)


def _proj_plain_kernel(xn_ref, w_ref, o_ref, ob_ref):
    v = _dot(xn_ref[...], w_ref[...])
    o_ref[...] = v
    ob_ref[...] = v.astype(ob_ref.dtype)


def _proj_plain(xn, w, blk, width):
    n, d = xn.shape
    tm = _tile(n, 512)
    row = lambda wd: pl.BlockSpec((tm, wd), lambda i: (i, 0))
    return pl.pallas_call(
        _proj_plain_kernel,
        grid=(n // tm,),
        in_specs=[row(d), pl.BlockSpec((d, width), lambda i: (0, blk))],
        out_specs=[row(width), row(width)],
        out_shape=[jax.ShapeDtypeStruct((n, width), F32), jax.ShapeDtypeStruct((n, width), BF16)],
        compiler_params=_params("parallel"),
        name="proj_plain",
    )(xn, w)


def _proj_gate_kernel(xn_ref, w_ref, o_ref):
    o_ref[...] = jax.nn.sigmoid(_dot(xn_ref[...], w_ref[...]))


def _proj_gate(xn, w, blk0, bw, nblk):
    n, d = xn.shape
    tm = _tile(n, 512)
    return pl.pallas_call(
        _proj_gate_kernel,
        grid=(n // tm, nblk),
        in_specs=[pl.BlockSpec((tm, d), lambda i, j: (i, 0)),
                  pl.BlockSpec((d, bw), lambda i, j: (0, blk0 + j))],
        out_specs=pl.BlockSpec((tm, bw), lambda i, j: (i, j)),
        out_shape=jax.ShapeDtypeStruct((n, bw * nblk), F32),
        compiler_params=_params("parallel", "parallel"),
        name="proj_gate",
    )(xn, w)


def _sb_attn_kernel(q_ref, k_ref, v_ref, o_ref, *, tq, hd, pairs):
    qi = pl.program_id(2)
    per = LANES // hd
    row = lax.broadcasted_iota(jnp.int32, (tq, tq), 0)
    col = lax.broadcasted_iota(jnp.int32, (tq, tq), 1)
    causal = col < row
    after_mat = (row > col).astype(BF16)
    lane_head = lax.broadcasted_iota(jnp.int32, (tq, LANES), 1) // hd

    def block(j, carries, accs, masked):
        ks = pl.multiple_of(j * tq, tq)
        new_carries, new_accs = [], []
        for p in range(pairs):
            kp = k_ref[pl.ds(ks, tq), p * LANES:(p + 1) * LANES]
            vp = v_ref[pl.ds(ks, tq), p * LANES:(p + 1) * LANES]
            acc = accs[p]
            for s in range(per):
                h = p * per + s
                z = _dot_nt(q_ref[:, h * LANES:(h + 1) * LANES], kp)
                sp = _softplus_neg_abs(z)
                log_sig = jnp.minimum(z, 0.0) - sp
                log_keep = jnp.minimum(-z, 0.0) - sp
                if masked:
                    log_keep = jnp.where(causal, log_keep, 0.0)
                hi = log_keep.astype(BF16)
                lo = (log_keep - hi.astype(F32)).astype(BF16)
                after = _dot(hi, after_mat) + _dot(lo, after_mat) + carries[h]
                w = jnp.exp(log_sig + after)
                if masked:
                    w = jnp.where(causal, w, 0.0)
                acc = acc + jnp.where(lane_head == s, _dot(w.astype(BF16), vp), 0.0)
                new_carries.append(carries[h] + jnp.sum(log_keep, axis=1, keepdims=True))
            new_accs.append(acc)
        return tuple(new_carries), tuple(new_accs)

    carries = tuple(jnp.zeros((tq, 1), F32) for _ in range(pairs * per))
    accs = tuple(jnp.zeros((tq, LANES), F32) for _ in range(pairs))
    carries, accs = block(qi, carries, accs, True)
    carries, accs = lax.fori_loop(0, qi, lambda t, c: block(qi - 1 - t, c[0], c[1], False), (carries, accs))
    for p in range(pairs):
        o_ref[:, p * LANES:(p + 1) * LANES] = accs[p].astype(o_ref.dtype)


def _sb_attn(q, k, v, bsz, t, hd, pairs):
    n, width = k.shape
    per = LANES // hd
    tq = _tile(t, ATTN_TQ)
    nq = t // tq
    gw = pairs * LANES
    assert width % gw == 0 and q.shape[1] == width * per
    return pl.pallas_call(
        functools.partial(_sb_attn_kernel, tq=tq, hd=hd, pairs=pairs),
        grid=(bsz, width // gw, nq),
        in_specs=[pl.BlockSpec((tq, gw * per), lambda b, h, i: (b * nq + i, h)),
                  pl.BlockSpec((t, gw), lambda b, h, i: (b, h)),
                  pl.BlockSpec((t, gw), lambda b, h, i: (b, h))],
        out_specs=pl.BlockSpec((tq, gw), lambda b, h, i: (b * nq + i, h)),
        out_shape=jax.ShapeDtypeStruct((n, width), BF16),
        compiler_params=_params("parallel", "parallel", "parallel"),
        name="sb_attn",
    )(q, k, v)


def _da_lambda(lam_ref, lam_init):
    lv = lam_ref[...]
    s1 = jnp.sum(lv[0:1] * lv[1:2], axis=1, keepdims=True)
    s2 = jnp.sum(lv[2:3] * lv[3:4], axis=1, keepdims=True)
    return jnp.exp(s1) - jnp.exp(s2) + lam_init


def _da_attn_kernel(lam_ref, g_ref, q_ref, k_ref, v_ref, o_ref, *, tq, heads, lam_init):
    qi = pl.program_id(2)
    row = lax.broadcasted_iota(jnp.int32, (tq, tq), 0)
    col = lax.broadcasted_iota(jnp.int32, (tq, tq), 1)
    causal = col <= row

    def block(j, state, masked):
        ks = pl.multiple_of(j * tq, tq)
        out = []
        for h in range(heads):
            kp = k_ref[pl.ds(ks, tq), h * LANES:(h + 1) * LANES]
            vv = v_ref[pl.ds(ks, tq), h * LANES:(h + 1) * LANES]
            for s in range(2):
                m, l, acc = state[2 * h + s]
                sc = _dot_nt(q_ref[:, (2 * h + s) * LANES:(2 * h + s + 1) * LANES], kp)
                if masked:
                    sc = jnp.where(causal, sc, NEG_INF)
                m_new = jnp.maximum(m, jnp.max(sc, axis=1, keepdims=True))
                alpha = jnp.exp(m - m_new)
                p = jnp.exp(sc - m_new)
                l = alpha * l + jnp.sum(p, axis=1, keepdims=True)
                acc = alpha * acc + _dot(p.astype(BF16), vv)
                out.append((m_new, l, acc))
        return tuple(out)

    init = tuple((jnp.full((tq, 1), NEG_INF, F32), jnp.zeros((tq, 1), F32), jnp.zeros((tq, LANES), F32))
                 for _ in range(2 * heads))
    state = lax.fori_loop(0, qi, lambda j, st: block(j, st, False), init)
    state = block(qi, state, True)
    lam = _da_lambda(lam_ref, lam_init)
    for h in range(heads):
        (_, l1, a1), (_, l2, a2) = state[2 * h], state[2 * h + 1]
        o = a1 / l1 - lam * (a2 / l2)
        o_ref[:, h * LANES:(h + 1) * LANES] = (_rms(o, g_ref[...], SUBLN_EPS) * (1.0 - lam_init)).astype(o_ref.dtype)


def _da_attn(lam_vecs, g, q, k, v, bsz, t, lam_init, heads):
    n, width = k.shape
    tq = _tile(t, ATTN_TQ)
    nq = t // tq
    gw = heads * LANES
    assert width % gw == 0 and q.shape[1] == 2 * width and v.shape[1] == width
    return pl.pallas_call(
        functools.partial(_da_attn_kernel, tq=tq, heads=heads, lam_init=lam_init),
        grid=(bsz, width // gw, nq),
        in_specs=[pl.BlockSpec(lam_vecs.shape, lambda b, h, i: (0, 0)),
                  pl.BlockSpec((1, LANES), lambda b, h, i: (0, 0)),
                  pl.BlockSpec((tq, 2 * gw), lambda b, h, i: (b * nq + i, h)),
                  pl.BlockSpec((t, gw), lambda b, h, i: (b, h)),
                  pl.BlockSpec((t, gw), lambda b, h, i: (b, h))],
        out_specs=pl.BlockSpec((tq, gw), lambda b, h, i: (b * nq + i, h)),
        out_shape=jax.ShapeDtypeStruct((n, width), BF16),
        compiler_params=_params("parallel", "parallel", "parallel"),
        name="da_attn",
    )(lam_vecs, g, q, k, v)


def _lru_gates(xc, wa_ref, ba_ref, wx_ref, bx_ref, lam_ref):
    xb = xc.astype(BF16)
    r = jax.nn.sigmoid(_dot(xb, wa_ref[...]) + ba_ref[...])
    i = jax.nn.sigmoid(_dot(xb, wx_ref[...]) + bx_ref[...])
    lam = lam_ref[...]
    log_sig_lam = jnp.minimum(lam, 0.0) - _softplus_neg_abs(lam)
    log_a = LRU_C * r * log_sig_lam
    a = jnp.exp(log_a)
    u = jnp.sqrt(1.0 - jnp.exp(2.0 * log_a)) * i * xc
    return a, u


def _lru_kernel(x_ref, cw_ref, cb_ref, wa_ref, ba_ref, wx_ref, bx_ref, lam_ref,
                y_ref, hl_ref, xe_ref, h_ref, *, tc):
    @pl.when(pl.program_id(1) == 0)
    def _():
        xe_ref[0:SUBLANES, :] = jnp.zeros((SUBLANES, xe_ref.shape[1]), F32)
        h_ref[...] = jnp.zeros_like(h_ref)

    xe_ref[SUBLANES:SUBLANES + tc, :] = x_ref[...]
    cw = cw_ref[...]
    xc = cb_ref[...]
    for j in range(cw.shape[0]):
        off = SUBLANES - (cw.shape[0] - 1) + j
        xc = xc + xe_ref[off:off + tc, :] * cw[j:j + 1]
    xe_ref[0:SUBLANES, :] = xe_ref[tc:tc + SUBLANES, :]

    a, u = _lru_gates(xc, wa_ref, ba_ref, wx_ref, bx_ref, lam_ref)
    rowi = lax.broadcasted_iota(jnp.int32, a.shape, 0)
    s = 1
    while s < tc:
        keep = rowi >= s
        a_sh = jnp.where(keep, pltpu.roll(a, s, axis=0), 1.0)
        u_sh = jnp.where(keep, pltpu.roll(u, s, axis=0), 0.0)
        u = a * u_sh + u
        a = a * a_sh
        s *= 2
    h = a * h_ref[...] + u
    y_ref[...] = h.astype(y_ref.dtype)
    h_last = h[tc - 1:tc, :]
    h_ref[...] = h_last
    hl_ref[...] = h_last


def _lru(x, bsz, t, cw, cb, wa, ba, wx, bx, lam):
    n, w = x.shape
    tc = _tile(t, 256)
    nt = t // tc
    full = lambda a: pl.BlockSpec(a.shape, lambda b, i: (0,) * a.ndim)
    vec = lambda a: a.reshape(1, w)
    args = (cw, vec(cb), wa, vec(ba), wx, vec(bx), vec(lam))
    return pl.pallas_call(
        functools.partial(_lru_kernel, tc=tc),
        grid=(bsz, nt),
        in_specs=[pl.BlockSpec((tc, w), lambda b, i: (b * nt + i, 0))] + [full(a) for a in args],
        out_specs=[pl.BlockSpec((tc, w), lambda b, i: (b * nt + i, 0)),
                   pl.BlockSpec((None, 1, w), lambda b, i: (b, 0, 0))],
        out_shape=[jax.ShapeDtypeStruct((n, w), BF16), jax.ShapeDtypeStruct((bsz, 1, w), F32)],
        scratch_shapes=[pltpu.VMEM((tc + SUBLANES, w), F32), pltpu.VMEM((1, w), F32)],
        compiler_params=_params("arbitrary", "arbitrary"),
        name="lru",
    )(x, *args)


def _lru_step_kernel(x_ref, b0_ref, b1_ref, b2_ref, h0_ref, cw_ref, cb_ref, wa_ref, ba_ref, wx_ref, bx_ref,
                     lam_ref, y_ref, h_ref):
    cw = cw_ref[...]
    xc = cb_ref[...]
    for j, r in enumerate((b0_ref, b1_ref, b2_ref, x_ref)):
        xc = xc + r[...] * cw[j:j + 1]
    a, u = _lru_gates(xc, wa_ref, ba_ref, wx_ref, bx_ref, lam_ref)
    h = u + a * h0_ref[...]
    y_ref[...] = h.astype(y_ref.dtype)
    h_ref[...] = h


def _lru_step(x, bufs, h0, cw, cb, wa, ba, wx, bx, lam):
    m, w = x.shape
    vec = lambda a: a.reshape(1, w)
    return pl.pallas_call(
        _lru_step_kernel,
        out_shape=[jax.ShapeDtypeStruct((m, w), BF16), jax.ShapeDtypeStruct((m, w), F32)],
        compiler_params=pltpu.CompilerParams(vmem_limit_bytes=VMEM_LIMIT),
        name="lru_step",
    )(x, bufs[0], bufs[1], bufs[2], h0, cw, vec(cb), wa, vec(ba), wx, vec(bx), vec(lam))


def _merge_kernel(ya_ref, yb_ref, yc_ref, g_ref, wa_ref, wb_ref, wc_ref, o_ref, *, d):
    m = g_ref[:, 0:d] * _dot(ya_ref[...], wa_ref[...])
    m = m + g_ref[:, d:2 * d] * _dot(yb_ref[...], wb_ref[...])
    m = m + g_ref[:, 2 * d:3 * d] * _dot(yc_ref[...], wc_ref[...])
    o_ref[...] = m.astype(o_ref.dtype)


def _merge(ya, yb, yc, g, wa, wb, wc):
    n = ya.shape[0]
    d = wa.shape[1]
    tm = _tile(n, 256)
    row = lambda a: pl.BlockSpec((tm, a.shape[1]), lambda i: (i, 0))
    full = lambda a: pl.BlockSpec(a.shape, lambda i: (0, 0))
    return pl.pallas_call(
        functools.partial(_merge_kernel, d=d),
        grid=(n // tm,),
        in_specs=[row(ya), row(yb), row(yc), row(g), full(wa), full(wb), full(wc)],
        out_specs=pl.BlockSpec((tm, d), lambda i: (i, 0)),
        out_shape=jax.ShapeDtypeStruct((n, d), BF16),
        compiler_params=_params("parallel"),
        name="merge",
    )(ya, yb, yc, g, wa, wb, wc)


def _outproj_kernel(m_ref, w_ref, x_ref, g_ref, xo_ref, xn_ref):
    xo = x_ref[...] + _dot(m_ref[...], w_ref[...])
    xo_ref[...] = xo
    xn_ref[...] = _rms(xo, g_ref[...], NORM_EPS).astype(xn_ref.dtype)


def _outproj(m, w, x, g):
    n, d = x.shape
    tm = _tile(n, 512)
    row = pl.BlockSpec((tm, d), lambda i: (i, 0))
    return pl.pallas_call(
        _outproj_kernel,
        grid=(n // tm,),
        in_specs=[row, pl.BlockSpec(w.shape, lambda i: (0, 0)), row, pl.BlockSpec((1, d), lambda i: (0, 0))],
        out_specs=[row, row],
        out_shape=[jax.ShapeDtypeStruct((n, d), F32), jax.ShapeDtypeStruct((n, d), BF16)],
        compiler_params=_params("parallel"),
        name="outproj",
    )(m, w, x, g.reshape(1, d))


def _ffn_epilogue(j, nj, part, x_ref, gn_ref, out_refs, acc_ref):
    @pl.when(j == 0)
    def _():
        acc_ref[...] = part

    @pl.when(j > 0)
    def _():
        acc_ref[...] += part

    @pl.when(j == nj - 1)
    def _():
        xo = x_ref[...] + acc_ref[...]
        if len(out_refs) == 2:
            out_refs[0][...] = xo
        out_refs[-1][...] = _rms(xo, gn_ref[...], NORM_EPS).astype(out_refs[-1].dtype)


def _ffn_kernel(xn_ref, wg_ref, wu_ref, wd_ref, cw_ref, cb_ref, x_ref, gn_ref, *rest,
                tm, tiles_per_seq, nj, n_out):
    out_refs = rest[:n_out]
    tail_ref, acc_ref, carry_ref = rest[n_out:n_out + 3]
    ge_refs = rest[n_out + 3:]
    ch = ge_refs[0].shape[1]
    i = pl.program_id(0)
    j = pl.program_id(1)
    xn = xn_ref[...]
    seq_start = (i % tiles_per_seq) == 0

    @pl.when(seq_start)
    def _():
        for ge_ref in ge_refs:
            ge_ref[0:SUBLANES, :] = jnp.zeros((SUBLANES, ch), F32)

    @pl.when(jnp.logical_not(seq_start))
    def _():
        for c, ge_ref in enumerate(ge_refs):
            ge_ref[0:SUBLANES, :] = carry_ref[j, :, c * ch:(c + 1) * ch]

    cw = cw_ref[...]
    cb = cb_ref[...]
    ups = []
    for c, ge_ref in enumerate(ge_refs):
        cs = slice(c * ch, (c + 1) * ch)
        ge_ref[SUBLANES:SUBLANES + tm, :] = _dot(xn, wg_ref[:, cs])
        ups.append(_dot(xn, wu_ref[:, cs]))
    part = None
    for c, ge_ref in enumerate(ge_refs):
        cs = slice(c * ch, (c + 1) * ch)
        tail = ge_ref[tm:tm + SUBLANES, :]
        carry_ref[j, :, cs] = tail
        tail_ref[:, cs] = tail
        g = cb[:, cs]
        for k in range(cw.shape[0]):
            off = SUBLANES - (cw.shape[0] - 1) + k
            g = g + ge_ref[off:off + tm, :] * cw[k:k + 1, cs]
        act = (g * jax.nn.sigmoid(g) * ups[c]).astype(BF16)
        d = _dot(act, wd_ref[cs, :])
        part = d if part is None else part + d
    _ffn_epilogue(j, nj, part, x_ref, gn_ref, out_refs, acc_ref)


def _ffn(xn, wg, wu, wd, cw, cb, x, g_next, bsz, t, emit_x, norm_dtype):
    n, d = x.shape
    f = wg.shape[1]
    tm = _tile(t, 512)
    tf = _tile(f, 512)
    ch = FFN_CHUNK if tf % FFN_CHUNK == 0 else tf
    nj = f // tf
    tiles_per_seq = t // tm
    row = pl.BlockSpec((tm, d), lambda i, j: (i, 0))
    out_specs = ([row] if emit_x else []) + [row]
    out_shape = ([jax.ShapeDtypeStruct((n, d), F32)] if emit_x else []) + [jax.ShapeDtypeStruct((n, d), norm_dtype)]
    out_specs.append(pl.BlockSpec((None, SUBLANES, tf), lambda i, j: (i, 0, j)))
    out_shape.append(jax.ShapeDtypeStruct((n // tm, SUBLANES, f), F32))
    return pl.pallas_call(
        functools.partial(_ffn_kernel, tm=tm, tiles_per_seq=tiles_per_seq, nj=nj, n_out=len(out_specs) - 1),
        grid=(n // tm, nj),
        in_specs=[row,
                  pl.BlockSpec((d, tf), lambda i, j: (0, j)), pl.BlockSpec((d, tf), lambda i, j: (0, j)),
                  pl.BlockSpec((tf, d), lambda i, j: (j, 0)),
                  pl.BlockSpec((cw.shape[0], tf), lambda i, j: (0, j)), pl.BlockSpec((1, tf), lambda i, j: (0, j)),
                  row, pl.BlockSpec((1, d), lambda i, j: (0, 0))],
        out_specs=out_specs,
        out_shape=out_shape,
        scratch_shapes=[pltpu.VMEM((tm, d), F32), pltpu.VMEM((nj, SUBLANES, tf), F32)]
                       + [pltpu.VMEM((tm + SUBLANES, ch), F32)] * (tf // ch),
        compiler_params=_params("arbitrary", "arbitrary"),
        name="ffn",
    )(xn, wg, wu, wd, cw, cb.reshape(1, f), x, g_next.reshape(1, d))


def _ffn_step_kernel(xn_ref, wg_ref, wu_ref, wd_ref, cw_ref, cb_ref, s0_ref, s1_ref, x_ref, gn_ref, *rest,
                     nj, n_out):
    out_refs = rest[:n_out]
    gp_ref, acc_ref = rest[n_out:]
    j = pl.program_id(0)
    xn = xn_ref[...]
    gp = _dot(xn, wg_ref[...])
    gp_ref[...] = gp
    cw = cw_ref[...]
    g = cb_ref[...]
    for c, r in enumerate((s0_ref[...], s1_ref[...], gp)):
        g = g + r * cw[c:c + 1]
    act = (g * jax.nn.sigmoid(g) * _dot(xn, wu_ref[...])).astype(BF16)
    _ffn_epilogue(j, nj, _dot(act, wd_ref[...]), x_ref, gn_ref, out_refs, acc_ref)


def _ffn_step(xn, wg, wu, wd, cw, cb, s0, s1, x, g_next, emit_x, norm_dtype):
    m, d = x.shape
    f = wg.shape[1]
    tf = _tile(f, 512)
    nj = f // tf
    row = pl.BlockSpec((m, d), lambda j: (0, 0))
    col = pl.BlockSpec((m, tf), lambda j: (0, j))
    out_specs = ([row] if emit_x else []) + [row, col]
    out_shape = (([jax.ShapeDtypeStruct((m, d), F32)] if emit_x else [])
                 + [jax.ShapeDtypeStruct((m, d), norm_dtype), jax.ShapeDtypeStruct((m, f), F32)])
    return pl.pallas_call(
        functools.partial(_ffn_step_kernel, nj=nj, n_out=len(out_specs) - 1),
        grid=(nj,),
        in_specs=[row,
                  pl.BlockSpec((d, tf), lambda j: (0, j)), pl.BlockSpec((d, tf), lambda j: (0, j)),
                  pl.BlockSpec((tf, d), lambda j: (j, 0)),
                  pl.BlockSpec((cw.shape[0], tf), lambda j: (0, j)), pl.BlockSpec((1, tf), lambda j: (0, j)),
                  col, col, row, pl.BlockSpec((1, d), lambda j: (0, 0))],
        out_specs=out_specs,
        out_shape=out_shape,
        scratch_shapes=[pltpu.VMEM((m, d), F32)],
        compiler_params=_params("arbitrary"),
        name="ffn_step",
    )(xn, wg, wu, wd, cw, cb.reshape(1, f), s0, s1, x, g_next.reshape(1, d))


def _head_select(rows, width, group):
    sub = lax.broadcasted_iota(jnp.int32, (rows, width), 0)
    lane = lax.broadcasted_iota(jnp.int32, (rows, width), 1)
    return (lane // group) == sub


def _block_diag_q(q_ref, rows, hd):
    width = q_ref.shape[-1]
    sel = _head_select(rows, width, hd)
    return jnp.where(sel, jnp.broadcast_to(q_ref[...].astype(F32), (rows, width)), 0.0).astype(BF16)


def _dec_sb_kernel(pt_ref, q_ref, *rest, npg, hd, rows):
    k_refs = rest[:npg]
    v_refs = rest[npg:2 * npg]
    o_ref, carry_ref, acc_ref = rest[2 * npg:]
    g = pl.program_id(1)
    width = q_ref.shape[-1]
    page = k_refs[0].shape[-1]

    @pl.when(g == 0)
    def _():
        carry_ref[...] = jnp.zeros_like(carry_ref)
        acc_ref[...] = jnp.zeros_like(acc_ref)

    qbd = _block_diag_q(q_ref, rows, hd)
    order = list(reversed(range(npg)))
    z = jnp.concatenate([_dot(qbd, k_refs[p][...].astype(BF16)) for p in order], axis=0)
    sp = _softplus_neg_abs(z)
    log_sig = jnp.minimum(z, 0.0) - sp
    log_keep = jnp.minimum(-z, 0.0) - sp
    hi = log_keep.astype(BF16)
    lo = (log_keep - hi.astype(F32)).astype(BF16)
    r = lax.broadcasted_iota(jnp.int32, (page, 2 * page), 0)
    c = lax.broadcasted_iota(jnp.int32, (page, 2 * page), 1)
    sum_mat = ((r > c) | (c >= page)).astype(BF16)
    sums = _dot(hi, sum_mat) + _dot(lo, sum_mat)
    off = carry_ref[...]
    offs = []
    for i in range(npg):
        offs.append(off)
        off = off + sums[i * rows:(i + 1) * rows, page:]
    carry_ref[...] = off
    w = jnp.exp(log_sig + sums[:, :page] + jnp.concatenate(offs, axis=0)).astype(BF16)
    acc = acc_ref[...]
    for i, p in enumerate(order):
        acc = acc + _dot_nt(w[i * rows:(i + 1) * rows], v_refs[p][...].astype(BF16))
    acc_ref[...] = acc

    @pl.when(g == pl.num_programs(1) - 1)
    def _():
        sel = _head_select(rows, width, hd)
        o_ref[...] = jnp.sum(jnp.where(sel, acc, 0.0), axis=0, keepdims=True).astype(o_ref.dtype)


def _page_specs(layer, npg, ngroups, block, newest_first):
    specs = []
    for p in range(npg):
        if newest_first:
            imap = lambda b, g, pt, p=p: (layer, pt[b, (ngroups - 1 - g) * npg + p], 0, 0)
        else:
            imap = lambda b, g, pt, p=p: (layer, pt[b, g * npg + p], 0, 0)
        specs.append(pl.BlockSpec((None, None) + block, imap))
    return specs


def _pages_per_step(n_pages, pref):
    npg = min(n_pages, pref)
    while n_pages % npg:
        npg -= 1
    return npg


def _dec_sb(page_table, q, cache_kt, cache_vt, layer, hd):
    m, width = q.shape
    n_pages = page_table.shape[1]
    page = cache_kt.shape[3]
    assert page == LANES and cache_kt.shape[2] == width
    npg = _pages_per_step(n_pages, 16)
    ngroups = n_pages // npg
    rows = 2 * SUBLANES
    assert width // hd <= rows
    vec = pl.BlockSpec((None, 1, width), lambda b, g, pt: (b, 0, 0))
    grid_spec = pltpu.PrefetchScalarGridSpec(
        num_scalar_prefetch=1,
        grid=(m, ngroups),
        in_specs=[vec] + _page_specs(layer, npg, ngroups, (width, page), True)
                 + _page_specs(layer, npg, ngroups, (width, page), True),
        out_specs=vec,
        scratch_shapes=[pltpu.VMEM((rows, page), F32), pltpu.VMEM((rows, width), F32)],
    )
    out = pl.pallas_call(
        functools.partial(_dec_sb_kernel, npg=npg, hd=hd, rows=rows),
        grid_spec=grid_spec,
        out_shape=jax.ShapeDtypeStruct((m, 1, width), BF16),
        compiler_params=_params("parallel", "arbitrary"),
        name="dec_sb",
    )(page_table, q.reshape(m, 1, width), *([cache_kt] * npg), *([cache_vt] * npg))
    return out.reshape(m, width)


def _dec_da_kernel(pt_ref, lam_ref, g_ref, q_ref, kn_ref, vn_ref, *rest, npg, hd, heads, rows, lam_init):
    k_refs = rest[:npg]
    v_refs = rest[npg:2 * npg]
    o_ref, m_ref, l_ref, acc_ref = rest[2 * npg:]
    g = pl.program_id(1)
    page = k_refs[0].shape[-1]
    vd = acc_ref.shape[-1]
    sub = lax.broadcasted_iota(jnp.int32, (rows, vd), 0)
    qbd = _block_diag_q(q_ref, rows, hd)

    @pl.when(g == 0)
    def _():
        kn = jnp.broadcast_to(kn_ref[...], qbd.shape).astype(F32)
        m_ref[...] = jnp.sum(qbd.astype(F32) * kn, axis=1, keepdims=True)
        l_ref[...] = jnp.ones_like(l_ref)
        vn = vn_ref[...].astype(F32)
        acc0 = jnp.zeros((rows, vd), F32)
        for h in range(heads):
            acc0 = jnp.where(sub // 2 == h, jnp.broadcast_to(vn[h:h + 1], (rows, vd)), acc0)
        acc_ref[...] = acc0

    sc = jnp.concatenate([_dot(qbd, k_refs[p][...].astype(BF16)) for p in range(npg)], axis=1)
    m_old = m_ref[...]
    m_new = jnp.maximum(m_old, jnp.max(sc, axis=1, keepdims=True))
    alpha = jnp.exp(m_old - m_new)
    pw = jnp.exp(sc - m_new)
    m_ref[...] = m_new
    l_ref[...] = alpha * l_ref[...] + jnp.sum(pw, axis=1, keepdims=True)
    acc = alpha * acc_ref[...]
    for p in range(npg):
        pp = pw[:, p * page:(p + 1) * page]
        for h in range(heads):
            vh = v_refs[p][pl.ds(h, page, stride=heads), :].astype(BF16)
            acc = acc + _dot(jnp.where(sub // 2 == h, pp, 0.0).astype(BF16), vh)
    acc_ref[...] = acc

    @pl.when(g == pl.num_programs(1) - 1)
    def _():
        lam = _da_lambda(lam_ref, lam_init)
        accn = acc / l_ref[...]
        gain = g_ref[...]
        for h in range(heads):
            oh = accn[2 * h:2 * h + 1] - lam * accn[2 * h + 1:2 * h + 2]
            o_ref[:, h * vd:(h + 1) * vd] = (_rms(oh, gain, SUBLN_EPS) * (1.0 - lam_init)).astype(o_ref.dtype)


def _dec_da(page_table, lam_vecs, gain, q, k_new, v_new, cache_kt, cache_v, layer, hd, heads, lam_init):
    m, kwidth = q.shape
    vd = cache_v.shape[3]
    n_pages = page_table.shape[1]
    page = cache_kt.shape[3]
    assert page == LANES and vd == LANES and cache_kt.shape[2] == kwidth and cache_v.shape[2] == page * heads
    npg = _pages_per_step(n_pages, 8)
    ngroups = n_pages // npg
    rows = -(-(kwidth // hd) // (2 * SUBLANES)) * (2 * SUBLANES)
    kvec = pl.BlockSpec((None, 1, kwidth), lambda b, g, pt: (b, 0, 0))
    grid_spec = pltpu.PrefetchScalarGridSpec(
        num_scalar_prefetch=1,
        grid=(m, ngroups),
        in_specs=[pl.BlockSpec(lam_vecs.shape, lambda b, g, pt: (0, 0)),
                  pl.BlockSpec((1, vd), lambda b, g, pt: (0, 0)),
                  kvec, kvec, pl.BlockSpec((None, heads, vd), lambda b, g, pt: (b, 0, 0))]
                 + _page_specs(layer, npg, ngroups, (kwidth, page), False)
                 + _page_specs(layer, npg, ngroups, (page * heads, vd), False),
        out_specs=pl.BlockSpec((None, 1, heads * vd), lambda b, g, pt: (b, 0, 0)),
        scratch_shapes=[pltpu.VMEM((rows, 1), F32), pltpu.VMEM((rows, 1), F32), pltpu.VMEM((rows, vd), F32)],
    )
    out = pl.pallas_call(
        functools.partial(_dec_da_kernel, npg=npg, hd=hd, heads=heads, rows=rows, lam_init=lam_init),
        grid_spec=grid_spec,
        out_shape=jax.ShapeDtypeStruct((m, 1, heads * vd), BF16),
        compiler_params=_params("parallel", "arbitrary"),
        name="dec_da",
    )(page_table, lam_vecs, gain, q.reshape(m, 1, kwidth), k_new.reshape(m, 1, kwidth),
      v_new.reshape(m, heads, vd), *([cache_kt] * npg), *([cache_v] * npg))
    return out.reshape(m, heads * vd)


def _rope_tables(pos, hd):
    half = hd // 2
    inv_freq = ROPE_THETA ** (-jnp.arange(half, dtype=F32) / half)
    ang = pos.astype(F32)[:, None] * inv_freq[None, :]
    cos = jnp.cos(ang)
    sin = jnp.sin(ang)
    reps = LANES // hd
    cos_t = jnp.tile(jnp.concatenate([cos, cos], axis=-1), (1, reps))
    sin_t = jnp.tile(jnp.concatenate([-sin, sin], axis=-1), (1, reps))
    return cos_t, sin_t


def _block_diag(w):
    nb, bi, bo = w.shape
    eye = jnp.eye(nb, dtype=w.dtype)
    return (eye[:, None, :, None] * w[:, :, None, :]).reshape(nb * bi, nb * bo)


def kernel(x_prompt, x_sample, cache_sb_k, cache_sb_v, cache_da_k, cache_da_v, state_lru_h, state_lru_conv, state_ffn_conv, page_table, norm1_g, w_in, lru_conv_w, lru_conv_b, lru_wa, lru_ba, lru_wx, lru_bx, lru_lambda, da_lam_q1, da_lam_k1, da_lam_q2, da_lam_k2, da_subln_g, w_br_a, w_br_b, w_br_c, w_out, norm2_g, ffn_w_gate, ffn_w_up, ffn_conv_w, ffn_conv_b, ffn_w_down, final_g):
    bsz, t, d = x_prompt.shape
    m = x_sample.shape[0]
    assert x_sample.shape[1] == 1
    depth, n_pool, page, sb_heads, sb_dim = cache_sb_k.shape
    da_sub, da_dim = cache_da_k.shape[3:]
    da_heads, da_vdim = cache_da_v.shape[3:]
    sbw = sb_heads * sb_dim
    lw = state_lru_h.shape[-1]
    daw = da_heads * da_vdim
    assert da_sub * da_dim == daw and da_vdim == LANES and LANES % sb_dim == 0 and LANES % da_dim == 0
    f = ffn_w_gate.shape[-1]
    past_len = page_table.shape[1] * page
    n = bsz * t

    blk_sb = 0
    off_da = 3 * sbw + lw
    assert off_da % (2 * daw) == 0 and (off_da + 2 * daw) % daw == 0 and (off_da + 3 * daw) % daw == 0
    blk_rope = off_da // (2 * daw)
    blk_dav = (off_da + 2 * daw) // daw
    gate_bw = daw
    assert (N_BRANCH * d) % gate_bw == 0
    blk_gate = (off_da + 3 * daw) // gate_bw
    n_gate = N_BRANCH * d // gate_bw

    cos_p, sin_p = _rope_tables(jnp.arange(t), da_dim)
    cos_s, sin_s = _rope_tables(jnp.full((m,), past_len), da_dim)
    tm_p = _tile(n, 512)
    assert t % tm_p == 0
    rope_blocks_p = t // tm_p

    to_kt = lambda c, w: jnp.transpose(c, (0, 1, 3, 4, 2)).reshape(depth, n_pool, w, page)
    ck_sb = to_kt(cache_sb_k, sbw)
    cv_sb = to_kt(cache_sb_v, sbw)
    ck_da = to_kt(cache_da_k, daw)
    cv_da = cache_da_v.reshape(depth, n_pool, page * da_heads, da_vdim)

    xp = x_prompt.reshape(n, d)
    xs = x_sample.reshape(m, d)
    xnp = _rmsnorm(xp, norm1_g[0], BF16)
    xns = _rmsnorm(xs, norm1_g[0], BF16)
    st_p, st_s = [], []
    for l in range(depth):
        lam_init = 0.8 - 0.6 * math.exp(-0.3 * l)
        last = l == depth - 1
        w_in_b = w_in[l].astype(BF16)
        wa_bd = _block_diag(lru_wa[l]).astype(BF16)
        wx_bd = _block_diag(lru_wx[l]).astype(BF16)
        wbr_a = w_br_a[l].astype(BF16)
        wbr_b = w_br_b[l].astype(BF16)
        wbr_c = w_br_c[l].astype(BF16)
        wo = w_out[l].astype(BF16)
        wg = ffn_w_gate[l].astype(BF16)
        wu = ffn_w_up[l].astype(BF16)
        wd = ffn_w_down[l].astype(BF16)
        lam_vecs = jnp.stack([da_lam_q1[l], da_lam_k1[l], da_lam_q2[l], da_lam_k2[l]])
        gain = da_subln_g[l].reshape(1, da_vdim)
        g_next = final_g if last else norm1_g[l + 1]
        norm_dtype = F32 if last else BF16
        lru_args = (lru_conv_w[l], lru_conv_b[l], wa_bd, lru_ba[l], wx_bd, lru_bx[l], lru_lambda[l])

        sbq, sbk, sbk_b, sbv, sbv_b, lru_in = _proj_sb(xnp, w_in_b, blk_sb, sbw, lw, sb_dim, True)
        daq, dak, dak_b = _proj_rope(xnp, w_in_b, blk_rope, daw, da_dim // 2, da_dim ** -0.5,
                                     cos_p, sin_p, rope_blocks_p, True)
        dav, dav_b = _proj_plain(xnp, w_in_b, blk_dav, daw)
        gates = _proj_gate(xnp, w_in_b, blk_gate, gate_bw, n_gate)
        y_sb = _sb_attn(sbq, sbk_b, sbv_b, bsz, t, sb_dim, sbw // LANES)
        y_da = _da_attn(lam_vecs, gain, daq, dak_b, dav_b, bsz, t, lam_init, da_heads // DA_GROUPS)
        y_lru, h_last = _lru(lru_in, bsz, t, *lru_args)
        merged = _merge(y_sb, y_lru, y_da, gates, wbr_a, wbr_b, wbr_c)
        xp, xn2 = _outproj(merged, wo, xp, norm2_g[l])
        outs = _ffn(xn2, wg, wu, wd, ffn_conv_w[l], ffn_conv_b[l], xp, g_next, bsz, t, not last, norm_dtype)
        if last:
            y_prompt, tail = outs
        else:
            xp, xnp, tail = outs
        kconv = lru_conv_w.shape[1] - 1
        kffn = ffn_conv_w.shape[1] - 1
        st_p.append((sbk.reshape(bsz, t, sb_heads, sb_dim), sbv.reshape(bsz, t, sb_heads, sb_dim),
                     dak.reshape(bsz, t, da_sub, da_dim), dav.reshape(bsz, t, da_heads, da_vdim),
                     h_last.reshape(bsz, lw), lru_in.reshape(bsz, t, lw)[:, t - kconv:],
                     tail[t // _tile(t, 512) - 1::t // _tile(t, 512), SUBLANES - kffn:]))

        sbq, sbk, _, sbv, _, lru_in = _proj_sb(xns, w_in_b, blk_sb, sbw, lw, sb_dim, False)
        daq, dak, dak_b = _proj_rope(xns, w_in_b, blk_rope, daw, da_dim // 2, da_dim ** -0.5, cos_s, sin_s, 0, False)
        dav, dav_b = _proj_plain(xns, w_in_b, blk_dav, daw)
        gates = _proj_gate(xns, w_in_b, blk_gate, gate_bw, n_gate)
        y_sb = _dec_sb(page_table, sbq, ck_sb, cv_sb, l, sb_dim)
        y_da = _dec_da(page_table, lam_vecs, gain, daq, dak_b, dav_b, ck_da, cv_da, l, da_dim, da_heads, lam_init)
        conv_state = state_lru_conv[l]
        y_lru, h_new = _lru_step(lru_in, [conv_state[:, c] for c in range(kconv)], state_lru_h[l], *lru_args)
        merged = _merge(y_sb, y_lru, y_da, gates, wbr_a, wbr_b, wbr_c)
        xs, xn2 = _outproj(merged, wo, xs, norm2_g[l])
        ffn_state = state_ffn_conv[l]
        outs = _ffn_step(xn2, wg, wu, wd, ffn_conv_w[l], ffn_conv_b[l], ffn_state[:, 0], ffn_state[:, 1],
                         xs, g_next, not last, norm_dtype)
        if last:
            y_sample, gp = outs
        else:
            xs, xns, gp = outs
        st_s.append((sbk.reshape(m, 1, sb_heads, sb_dim), sbv.reshape(m, 1, sb_heads, sb_dim),
                     dak.reshape(m, 1, da_sub, da_dim), dav.reshape(m, 1, da_heads, da_vdim),
                     h_new, jnp.concatenate([conv_state[:, 1:], lru_in[:, None]], axis=1),
                     jnp.concatenate([ffn_state[:, 1:], gp[:, None]], axis=1)))

    p_state = [jnp.stack(z) for z in zip(*st_p)]
    s_state = [jnp.stack(z) for z in zip(*st_s)]
    return (y_prompt.reshape(bsz, t, d), y_sample.reshape(m, 1, d), *p_state, *s_state)
```
